```python
import math
import jax, jax.numpy as jnp
from jax import lax
import numpy as np

D_MODEL = 2048
BATCH = 2
SEQ = 8192
DEPTH = 4

HEAD_DIM = 64
EPS = 1e-6
NEG = -1e30
A_WIDTH = D_MODEL // 2
A_BLOCKS = 16
A_BLOCK = A_WIDTH // A_BLOCKS
CONV_WIDTH = 4
RG_C = 8.0
B_HEADS = (D_MODEL // 2) // HEAD_DIM
B_WIDTH = B_HEADS * HEAD_DIM
DILATION_PAIRS = ((128, 1), (512, 4), (2048, 16))
EVEN_IN = 2 * A_WIDTH + 3 * B_WIDTH
EVEN_MIX = A_WIDTH + B_WIDTH
C_HEADS = D_MODEL // (2 * HEAD_DIM)
C_QK = HEAD_DIM
C_V = 2 * HEAD_DIM
ODD_IN = C_HEADS * (4 * C_QK + C_V)
ODD_MIX = C_HEADS * C_V
Q_BLOCK = 128
ROPE_THETA = 500000.0
ROT_DIM = HEAD_DIM // 4
D_FF = -(-8 * D_MODEL // (3 * 256)) * 256
N_EVEN = (DEPTH + 1) // 2
N_ODD = DEPTH // 2

kernel_name = "hybrid_rglru_dilated_diffattn_encoder"


def rms_norm(x, g):
    xf = x.astype(jnp.float32)
    y = xf * lax.rsqrt(jnp.mean(xf * xf, axis=-1, keepdims=True) + EPS)
    return (y * g.astype(jnp.float32)).astype(x.dtype)


def rope_tables(seq):
    pos = jnp.arange(seq, dtype=jnp.float32)
    inv = ROPE_THETA ** (-jnp.arange(0, ROT_DIM, 2, dtype=jnp.float32) / ROT_DIM)
    ang = pos[:, None] * inv[None, :]
    return jnp.cos(ang), jnp.sin(ang)


def apply_partial_rope(x, cos, sin):
    half = ROT_DIM // 2
    shp = (1, x.shape[1]) + (1,) * (x.ndim - 3) + (half,)
    c = cos.reshape(shp)
    s = sin.reshape(shp)
    x1 = x[..., :half].astype(jnp.float32)
    x2 = x[..., half:ROT_DIM].astype(jnp.float32)
    return jnp.concatenate([x1 * c - x2 * s, x2 * c + x1 * s,
                            x[..., ROT_DIM:].astype(jnp.float32)], axis=-1)


def _lin_combine(left, right):
    a_l, b_l = left
    a_r, b_r = right
    return a_l * a_r, a_r * b_l + b_r


def linear_scan(a, u, reverse):
    _, h = lax.associative_scan(_lin_combine, (a, u), reverse=reverse, axis=1)
    return h


def bidirectional_rglru(xc, w_r, b_r, w_i, b_i, lam):
    bsz, seq, _ = xc.shape
    xf = xc.astype(jnp.float32)
    xb = xf.reshape(bsz, seq, A_BLOCKS, A_BLOCK)
    r = jax.nn.sigmoid(jnp.einsum('bsne,dnef->dbsnf', xb, w_r.astype(jnp.float32)).reshape(2, bsz, seq, A_WIDTH)
                       + b_r.astype(jnp.float32)[:, None, None, :])
    i = jax.nn.sigmoid(jnp.einsum('bsne,dnef->dbsnf', xb, w_i.astype(jnp.float32)).reshape(2, bsz, seq, A_WIDTH)
                       + b_i.astype(jnp.float32)[:, None, None, :])
    log_a = -RG_C * r * jax.nn.softplus(-lam.astype(jnp.float32))[:, None, None, :]
    a = jnp.exp(log_a)
    u = jnp.sqrt(jnp.maximum(-jnp.expm1(2.0 * log_a), 0.0)) * (i * xf[None])
    h_fwd = linear_scan(a[0], u[0], reverse=False)
    h_bwd = linear_scan(a[1], u[1], reverse=True)
    return h_fwd + h_bwd


def dilated_branch(q, k, v, window, dilation):
    bsz, seq, nh, dh = q.shape
    half = window // (2 * dilation)
    blk = half
    L = seq // dilation
    nb = -(-L // blk)
    pad = nb * blk - L

    def by_residue(t):
        return t.reshape(bsz, L, dilation, nh, dh).transpose(0, 2, 1, 3, 4)

    def neighbours(t):
        tp = jnp.pad(t, ((0, 0), (0, 0), (blk, pad + blk), (0, 0), (0, 0)))
        tp = tp.reshape(bsz, dilation, nb + 2, blk, nh, dh)
        return jnp.concatenate([tp[:, :, :-2], tp[:, :, 1:-1], tp[:, :, 2:]], axis=3)

    qb = jnp.pad(by_residue(q), ((0, 0), (0, 0), (0, pad), (0, 0), (0, 0)))
    qb = qb.reshape(bsz, dilation, nb, blk, nh, dh)
    kb = neighbours(by_residue(k))
    vb = neighbours(by_residue(v))
    s = jnp.einsum('bdnqhe,bdnkhe->bdnhqk', qb, kb)
    rel = jnp.arange(3 * blk)[None, :] - blk - jnp.arange(blk)[:, None]
    band = jnp.abs(rel) <= half
    key_u = jnp.arange(nb)[:, None] * blk - blk + jnp.arange(3 * blk)[None, :]
    inrange = (key_u >= 0) & (key_u < L)
    valid = band[None, :, :] & inrange[:, None, :]
    s = jnp.where(valid[None, None, :, None, :, :], s, NEG)
    m = jnp.max(s, axis=-1, keepdims=True)
    p = jnp.exp(s - m)
    den = jnp.sum(p, axis=-1, keepdims=True)
    o = jnp.einsum('bdnhqk,bdnkhe->bdnqhe', p, vb) / den.transpose(0, 1, 2, 4, 3, 5)
    lse = (m + jnp.log(den))[..., 0].transpose(0, 1, 2, 4, 3)

    def back(t):
        t = t.reshape((bsz, dilation, nb * blk) + t.shape[4:])[:, :, :L]
        t = jnp.swapaxes(t, 1, 2)
        return t.reshape((bsz, seq) + t.shape[3:])

    return back(o), back(lse)


def dilated_mixture_attention(q, k, v):
    outs, lses = [], []
    for window, dilation in DILATION_PAIRS:
        o, l = dilated_branch(q, k, v, window, dilation)
        outs.append(o)
        lses.append(l)
    wts = jax.nn.softmax(jnp.stack(lses, axis=0), axis=0)
    return jnp.sum(wts[..., None] * jnp.stack(outs, axis=0), axis=0)


def even_mixer(h, w_in, conv_w, conv_b, w_r, b_r, w_i, b_i, lam, w_out, cos, sin):
    bsz, seq, _ = h.shape
    z = h @ w_in
    o1, o2, o3, o4 = A_WIDTH, 2 * A_WIDTH, 2 * A_WIDTH + B_WIDTH, 2 * A_WIDTH + 2 * B_WIDTH
    gate, xr, q, k, v = z[..., :o1], z[..., o1:o2], z[..., o2:o3], z[..., o3:o4], z[..., o4:]
    xc = lax.conv_general_dilated(
        xr, conv_w[:, None, :], window_strides=(1,),
        padding=[(CONV_WIDTH // 2, CONV_WIDTH - 1 - CONV_WIDTH // 2)],
        dimension_numbers=('NWC', 'WIO', 'NWC'), feature_group_count=A_WIDTH) + conv_b
    y_a = bidirectional_rglru(xc, w_r, b_r, w_i, b_i, lam) * jax.nn.gelu(gate.astype(jnp.float32))
    q = apply_partial_rope(q.reshape(bsz, seq, B_HEADS, HEAD_DIM), cos, sin) * (HEAD_DIM ** -0.5)
    k = apply_partial_rope(k.reshape(bsz, seq, B_HEADS, HEAD_DIM), cos, sin)
    v = v.reshape(bsz, seq, B_HEADS, HEAD_DIM).astype(jnp.float32)
    y_b = dilated_mixture_attention(q, k, v).reshape(bsz, seq, B_WIDTH)
    y = jnp.concatenate([y_a, y_b], axis=-1).astype(w_out.dtype)
    return y @ w_out


def diff_attention_mixer(h, w_in, lq1, lk1, lq2, lk2, subln, w_out, lam_init, cos, sin):
    bsz, seq, _ = h.shape
    z = h @ w_in
    qk_w = C_HEADS * 2 * C_QK
    q = z[..., :qk_w].reshape(bsz, seq, C_HEADS, 2, C_QK)
    k = z[..., qk_w:2 * qk_w].reshape(bsz, seq, C_HEADS, 2, C_QK)
    v = z[..., 2 * qk_w:].reshape(bsz, seq, C_HEADS, C_V).astype(jnp.float32)
    q = apply_partial_rope(q, cos, sin) * (C_QK ** -0.5)
    k = apply_partial_rope(k, cos, sin)
    f32 = jnp.float32
    lam = (jnp.exp(jnp.sum(lq1.astype(f32) * lk1.astype(f32)))
           - jnp.exp(jnp.sum(lq2.astype(f32) * lk2.astype(f32))) + lam_init)
    nq = seq // Q_BLOCK
    qb = jnp.swapaxes(q.reshape(bsz, nq, Q_BLOCK, C_HEADS, 2, C_QK), 0, 1)

    def block(qblk):
        s = jnp.einsum('bqhje,bkhje->bhjqk', qblk, k)
        p = jax.nn.softmax(s, axis=-1)
        att = p[:, :, 0] - lam * p[:, :, 1]
        return jnp.einsum('bhqk,bkhe->bqhe', att, v)

    o = jnp.swapaxes(lax.map(block, qb), 0, 1).reshape(bsz, seq, C_HEADS, C_V)
    o = rms_norm(o, subln) * (1.0 - lam_init)
    return o.reshape(bsz, seq, ODD_MIX).astype(w_out.dtype) @ w_out


def swiglu(h, w_gate, w_up, w_down):
    return (jax.nn.silu(h @ w_gate) * (h @ w_up)) @ w_down


def setup_inputs(seed: int = 0) -> dict:
    key = jax.random.key(seed)
    ks = iter(jax.random.split(key, 40))

    def nrm(shape, scale):
        return jax.random.normal(next(ks), shape, jnp.float32) * scale

    def gain(shape):
        return 1.0 + nrm(shape, 0.02)

    x = nrm((BATCH, SEQ, D_MODEL), 1.0)
    norm_mix_pre = gain((DEPTH, D_MODEL))
    norm_mix_post = gain((DEPTH, D_MODEL))
    norm_ffn_pre = gain((DEPTH, D_MODEL))
    norm_ffn_post = gain((DEPTH, D_MODEL))
    ev_w_in = nrm((N_EVEN, D_MODEL, EVEN_IN), D_MODEL ** -0.5)
    ev_conv_w = nrm((N_EVEN, CONV_WIDTH, A_WIDTH), CONV_WIDTH ** -0.5)
    ev_conv_b = nrm((N_EVEN, A_WIDTH), 0.02)
    ev_w_r = nrm((N_EVEN, 2, A_BLOCKS, A_BLOCK, A_BLOCK), A_BLOCK ** -0.5)
    ev_b_r = nrm((N_EVEN, 2, A_WIDTH), 0.02)
    ev_w_i = nrm((N_EVEN, 2, A_BLOCKS, A_BLOCK, A_BLOCK), A_BLOCK ** -0.5)
    ev_b_i = nrm((N_EVEN, 2, A_WIDTH), 0.02)
    a_c = jax.random.uniform(next(ks), (N_EVEN, 2, A_WIDTH), jnp.float32, 0.9, 0.999)
    base = a_c ** (1.0 / RG_C)
    ev_lam = jnp.log(base) - jnp.log1p(-base)
    ev_w_out = nrm((N_EVEN, EVEN_MIX, D_MODEL), EVEN_MIX ** -0.5)
    od_w_in = nrm((N_ODD, D_MODEL, ODD_IN), D_MODEL ** -0.5)
    od_lam_q1 = nrm((N_ODD, C_QK), 0.1)
    od_lam_k1 = nrm((N_ODD, C_QK), 0.1)
    od_lam_q2 = nrm((N_ODD, C_QK), 0.1)
    od_lam_k2 = nrm((N_ODD, C_QK), 0.1)
    od_subln = gain((N_ODD, C_V))
    od_w_out = nrm((N_ODD, ODD_MIX, D_MODEL), ODD_MIX ** -0.5)
    ffn_w_gate = nrm((DEPTH, D_MODEL, D_FF), D_MODEL ** -0.5)
    ffn_w_up = nrm((DEPTH, D_MODEL, D_FF), D_MODEL ** -0.5)
    ffn_w_down = nrm((DEPTH, D_FF, D_MODEL), D_FF ** -0.5)
    return {"x": x, "norm_mix_pre": norm_mix_pre, "norm_mix_post": norm_mix_post,
            "norm_ffn_pre": norm_ffn_pre, "norm_ffn_post": norm_ffn_post,
            "ev_w_in": ev_w_in, "ev_conv_w": ev_conv_w, "ev_conv_b": ev_conv_b,
            "ev_w_r": ev_w_r, "ev_b_r": ev_b_r, "ev_w_i": ev_w_i, "ev_b_i": ev_b_i,
            "ev_lam": ev_lam, "ev_w_out": ev_w_out,
            "od_w_in": od_w_in, "od_lam_q1": od_lam_q1, "od_lam_k1": od_lam_k1,
            "od_lam_q2": od_lam_q2, "od_lam_k2": od_lam_k2, "od_subln": od_subln,
            "od_w_out": od_w_out,
            "ffn_w_gate": ffn_w_gate, "ffn_w_up": ffn_w_up, "ffn_w_down": ffn_w_down}


def reference(x, norm_mix_pre, norm_mix_post, norm_ffn_pre, norm_ffn_post,
              ev_w_in, ev_conv_w, ev_conv_b, ev_w_r, ev_b_r, ev_w_i, ev_b_i, ev_lam, ev_w_out,
              od_w_in, od_lam_q1, od_lam_k1, od_lam_q2, od_lam_k2, od_subln, od_w_out,
              ffn_w_gate, ffn_w_up, ffn_w_down):
    seq = x.shape[1]
    cos, sin = rope_tables(seq)
    for layer in range(DEPTH):
        h = rms_norm(x, norm_mix_pre[layer])
        j = layer // 2
        if layer % 2 == 0:
            m = even_mixer(h, ev_w_in[j], ev_conv_w[j], ev_conv_b[j], ev_w_r[j], ev_b_r[j],
                           ev_w_i[j], ev_b_i[j], ev_lam[j], ev_w_out[j], cos, sin)
        else:
            lam_init = 0.8 - 0.6 * math.exp(-0.3 * layer)
            m = diff_attention_mixer(h, od_w_in[j], od_lam_q1[j], od_lam_k1[j], od_lam_q2[j],
                                     od_lam_k2[j], od_subln[j], od_w_out[j], lam_init, cos, sin)
        x = x + rms_norm(m.astype(x.dtype), norm_mix_post[layer])
        h = rms_norm(x, norm_ffn_pre[layer])
        f = swiglu(h, ffn_w_gate[layer], ffn_w_up[layer], ffn_w_down[layer])
        x = x + rms_norm(f.astype(x.dtype), norm_ffn_post[layer])
    return x
```

```python
import functools
import math

import jax
import jax.numpy as jnp
from jax import lax
from jax.experimental import pallas as pl
from jax.experimental.pallas import tpu as pltpu

F32 = jnp.float32
BF16 = jnp.bfloat16

EPS = 1e-6
NEG = -1e30
HEAD_DIM = 64
ROT_DIM = HEAD_DIM // 4
ROT_HALF = ROT_DIM // 2
ROPE_THETA = 500000.0
CONV_WIDTH = 4
RG_C = 8.0
DILATION_PAIRS = ((128, 1), (512, 4), (2048, 16))
GELU_C = math.sqrt(2.0 / math.pi)

LANES = 128
SUBLANES = 8
MXU_DIM = 256
VMEM_LIMIT_BYTES = 56 * 1024 * 1024

_NT = (((1,), (1,)), ((), ()))


def _pick(n, prefs):
    for p in prefs:
        if n % p == 0:
            return p
    return n


def _cparams(sem):
    return pltpu.CompilerParams(dimension_semantics=sem, vmem_limit_bytes=VMEM_LIMIT_BYTES)


def _rms(x, g):
    return x * lax.rsqrt(jnp.mean(x * x, axis=-1, keepdims=True) + EPS) * g


def _norm_matmul_kernel(x_ref, g_ref, w_ref, o_ref, h_ref):
    @pl.when(pl.program_id(1) == 0)
    def _():
        h_ref[...] = _rms(x_ref[...], g_ref[...]).astype(BF16)

    o_ref[...] = jnp.dot(h_ref[...], w_ref[...], preferred_element_type=F32).astype(o_ref.dtype)


def norm_matmul(x, g, w, out_dtype):
    t, d = x.shape
    n = w.shape[1]
    tm = _pick(t, (512, 256, 128))
    tn = _pick(n, (1024, 512, 256, 128))
    return pl.pallas_call(
        _norm_matmul_kernel,
        grid=(t // tm, n // tn),
        in_specs=[pl.BlockSpec((tm, d), lambda i, j: (i, 0)),
                  pl.BlockSpec((1, d), lambda i, j: (0, 0)),
                  pl.BlockSpec((d, tn), lambda i, j: (0, j))],
        out_specs=pl.BlockSpec((tm, tn), lambda i, j: (i, j)),
        out_shape=jax.ShapeDtypeStruct((t, n), out_dtype),
        scratch_shapes=[pltpu.VMEM((tm, d), BF16)],
        compiler_params=_cparams(("parallel", "arbitrary")),
        name="norm_matmul",
    )(x, g, w)


def _qkv_kernel(x_ref, g_ref, w_ref, c_ref, sa_ref, sb_ref, o_ref, h_ref, *, n_q, n_rope):
    j = pl.program_id(1)

    @pl.when(j == 0)
    def _():
        h_ref[...] = _rms(x_ref[...], g_ref[...]).astype(BF16)

    acc = jnp.dot(h_ref[...], w_ref[...], preferred_element_type=F32)
    tn = acc.shape[1]

    @pl.when(j < n_rope)
    def _():
        scale = jnp.where(j < n_q, HEAD_DIM ** -0.5, 1.0).astype(F32)
        c = c_ref[...] * scale
        sa = sa_ref[...] * scale
        sb = sb_ref[...] * scale
        for cb in range(tn // LANES):
            a = acc[:, cb * LANES:(cb + 1) * LANES]
            r = (a * c + pltpu.roll(a, LANES - ROT_HALF, 1) * sa
                 + pltpu.roll(a, ROT_HALF, 1) * sb)
            o_ref[:, cb * LANES:(cb + 1) * LANES] = r.astype(o_ref.dtype)

    @pl.when(j >= n_rope)
    def _():
        o_ref[...] = acc.astype(o_ref.dtype)


def qkv_proj(x, g, w, rope, seq):
    t, d = x.shape
    n = w.shape[1]
    width = n // 3
    tm = _pick(seq, (512, 256, 128))
    tn = _pick(width, (1024, 512, 256, 128))
    n_q = width // tn
    c, sa, sb = rope
    nblk_s = seq // tm
    tab_spec = pl.BlockSpec((tm, LANES), lambda i, j: (i % nblk_s, 0))
    return pl.pallas_call(
        functools.partial(_qkv_kernel, n_q=n_q, n_rope=2 * n_q),
        grid=(t // tm, n // tn),
        in_specs=[pl.BlockSpec((tm, d), lambda i, j: (i, 0)),
                  pl.BlockSpec((1, d), lambda i, j: (0, 0)),
                  pl.BlockSpec((d, tn), lambda i, j: (0, j)),
                  tab_spec, tab_spec, tab_spec],
        out_specs=pl.BlockSpec((tm, tn), lambda i, j: (i, j)),
        out_shape=jax.ShapeDtypeStruct((t, n), BF16),
        scratch_shapes=[pltpu.VMEM((tm, d), BF16)],
        compiler_params=_cparams(("parallel", "arbitrary")),
        name="qkv_proj",
    )(x, g, w, c, sa, sb)


def rope_tables(seq):
    pos = jnp.arange(seq, dtype=F32)
    inv = ROPE_THETA ** (-jnp.arange(0, ROT_DIM, 2, dtype=F32) / ROT_DIM)
    ang = pos[:, None] * inv[None, :]
    cos, sin = jnp.cos(ang), jnp.sin(ang)
    zeros = jnp.zeros((seq, HEAD_DIM - ROT_DIM), F32)
    zh = jnp.zeros((seq, ROT_HALF), F32)
    c = jnp.concatenate([cos, cos, zeros + 1.0], axis=1)
    sa = jnp.concatenate([-sin, zh, zeros], axis=1)
    sb = jnp.concatenate([zh, sin, zeros], axis=1)
    rep = LANES // HEAD_DIM
    return tuple(jnp.tile(tab, (1, rep)) for tab in (c, sa, sb))


def _shift_rows(x, s, fill, reverse):
    n = x.shape[0]
    if s % SUBLANES == 0:
        pad = jnp.full((s, x.shape[1]), fill, x.dtype)
        return (jnp.concatenate([x[s:], pad], axis=0) if reverse
                else jnp.concatenate([pad, x[:n - s]], axis=0))
    row = lax.broadcasted_iota(jnp.int32, x.shape, 0)
    if reverse:
        return jnp.where(row < n - s, pltpu.roll(x, n - s, 0), fill)
    return jnp.where(row >= s, pltpu.roll(x, s, 0), fill)


def _rglru_kernel(*refs, reverse, final, n_blk):
    if final:
        (xp_ref, xc_ref, xn_ref, cw_ref, cb_ref, wg_ref, br_ref, bi_ref, lam_ref,
         gate_ref, prev_ref, o_ref, carry_ref) = refs
    else:
        (xp_ref, xc_ref, xn_ref, cw_ref, cb_ref, wg_ref, br_ref, bi_ref, lam_ref,
         o_ref, carry_ref) = refs
    i = pl.program_id(1)
    blk = n_blk - 1 - i if reverse else i
    tb, aw = xc_ref.shape

    @pl.when(i == 0)
    def _():
        carry_ref[...] = jnp.zeros_like(carry_ref)

    x0 = xc_ref[...]
    xp = jnp.where(blk > 0, xp_ref[...], 0.0)
    xn = jnp.where(blk < n_blk - 1, xn_ref[...], 0.0)
    row = lax.broadcasted_iota(jnp.int32, (tb, aw), 0)
    zpad = jnp.zeros((tb - SUBLANES, aw), F32)
    conv = cb_ref[...] + cw_ref[2:3, :] * x0
    for back in (1, 2):
        head = jnp.concatenate([pltpu.roll(xp, back, 0), zpad], axis=0)
        shifted = jnp.where(row < back, head, pltpu.roll(x0, back, 0))
        conv = conv + cw_ref[2 - back:3 - back, :] * shifted
    tail = jnp.concatenate([zpad, pltpu.roll(xn, SUBLANES - 1, 0)], axis=0)
    shifted = jnp.where(row == tb - 1, tail, pltpu.roll(x0, tb - 1, 0))
    conv = conv + cw_ref[3:4, :] * shifted

    n_grp, gw, _ = wg_ref.shape
    cbf = conv.astype(BF16)
    zr, zi = [], []
    for g in range(n_grp):
        z = jnp.dot(cbf[:, g * gw:(g + 1) * gw], wg_ref[g], preferred_element_type=F32)
        zr.append(z[:, :gw])
        zi.append(z[:, gw:])
    zr = jnp.concatenate(zr, axis=1) if n_grp > 1 else zr[0]
    zi = jnp.concatenate(zi, axis=1) if n_grp > 1 else zi[0]
    r = 1.0 / (1.0 + jnp.exp(-(zr + br_ref[...])))
    ig = 1.0 / (1.0 + jnp.exp(-(zi + bi_ref[...])))
    nl = -lam_ref[...]
    softplus = jnp.maximum(nl, 0.0) + jnp.log(1.0 + jnp.exp(-jnp.abs(nl)))
    a = jnp.exp(-RG_C * r * softplus)
    u = jnp.sqrt(jnp.maximum(1.0 - a * a, 0.0)) * (ig * conv)

    s = 1
    while s < tb:
        a_sh = _shift_rows(a, s, 1.0, reverse)
        u_sh = _shift_rows(u, s, 0.0, reverse)
        u = a * u_sh + u
        a = a * a_sh
        s *= 2
    edge = 0 if reverse else SUBLANES - 1
    h = a * carry_ref[edge:edge + 1, :] + u
    carry_ref[...] = h[:SUBLANES] if reverse else h[tb - SUBLANES:]

    if final:
        gt = gate_ref[...]
        gelu = 0.5 * gt * (1.0 + jnp.tanh(GELU_C * (gt + 0.044715 * (gt * gt * gt))))
        o_ref[...] = ((prev_ref[...] + h) * gelu).astype(o_ref.dtype)
    else:
        o_ref[...] = h


def rglru_direction(z_a, conv_w, conv_b, w_gate, b_r, b_i, lam, prev, reverse):
    bsz, seq, two_aw = z_a.shape
    aw = two_aw // 2
    tb = _pick(seq, (256, 128, 64, 32, 16, 8))
    n_blk = seq // tb
    rows8 = seq // SUBLANES
    per8 = tb // SUBLANES
    final = prev is not None

    def blk(i):
        return n_blk - 1 - i if reverse else i

    xr_specs = [
        pl.BlockSpec((None, SUBLANES, aw), lambda b, i: (b, jnp.maximum(blk(i) * per8 - 1, 0), 1)),
        pl.BlockSpec((None, tb, aw), lambda b, i: (b, blk(i), 1)),
        pl.BlockSpec((None, SUBLANES, aw),
                     lambda b, i: (b, jnp.minimum((blk(i) + 1) * per8, rows8 - 1), 1)),
    ]
    full = lambda a: pl.BlockSpec(a.shape, lambda b, i: (0,) * a.ndim)
    params = [conv_w, conv_b, w_gate, b_r, b_i, lam]
    in_specs = xr_specs + [full(p) for p in params]
    args = [z_a, z_a, z_a] + params
    if final:
        in_specs += [pl.BlockSpec((None, tb, aw), lambda b, i: (b, blk(i), 0)),
                     pl.BlockSpec((None, tb, aw), lambda b, i: (b, blk(i), 0))]
        args += [z_a, prev]
    return pl.pallas_call(
        functools.partial(_rglru_kernel, reverse=reverse, final=final, n_blk=n_blk),
        grid=(bsz, n_blk),
        in_specs=in_specs,
        out_specs=pl.BlockSpec((None, tb, aw), lambda b, i: (b, blk(i), 0)),
        out_shape=jax.ShapeDtypeStruct((bsz, seq, aw), BF16 if final else F32),
        scratch_shapes=[pltpu.VMEM((SUBLANES, aw), F32)],
        compiler_params=_cparams(("parallel", "arbitrary")),
        name="rglru_bwd" if reverse else "rglru_fwd",
    )(*args)


def gate_group_weights(w_r, w_i):
    n, bs, _ = w_r.shape
    aw = n * bs
    gw = _pick(aw, (MXU_DIM, LANES))
    per = gw // bs
    n_grp = aw // gw
    eye = jnp.eye(per, dtype=w_r.dtype)

    def dense(w):
        wg = w.reshape(n_grp, per, bs, bs)
        return jnp.einsum('gaef,ab->gaebf', wg, eye).reshape(n_grp, gw, gw)

    return jnp.concatenate([dense(w_r), dense(w_i)], axis=2).astype(BF16)


def _dilated_kernel(*refs, has_prev, final, half, length):
    q_ref, kp_ref, kc_ref, kn_ref, vp_ref, vc_ref, vn_ref = refs[:7]
    rest = refs[7:]
    if has_prev:
        po_ref, pl_ref = rest[:2]
        rest = rest[2:]
    o_ref = rest[0]
    lse_ref = None if final else rest[1]
    i = pl.program_id(2)
    tu, width = q_ref.shape
    nk = tu + 2 * half

    rowq = lax.broadcasted_iota(jnp.int32, (tu, nk), 0)
    colk = lax.broadcasted_iota(jnp.int32, (tu, nk), 1)
    rel = colk - half - rowq
    key_u = i * tu - half + colk
    valid = (jnp.abs(rel) <= half) & (key_u >= 0) & (key_u < length)
    lane = lax.broadcasted_iota(jnp.int32, (tu, LANES), 1)
    lo = lane < HEAD_DIM

    for hp in range(width // LANES):
        sl = slice(hp * LANES, (hp + 1) * LANES)
        q = q_ref[:, sl]
        k = jnp.concatenate([kp_ref[:, sl], kc_ref[:, sl], kn_ref[:, sl]], axis=0)
        v = jnp.concatenate([vp_ref[:, sl], vc_ref[:, sl], vn_ref[:, sl]], axis=0)
        outs, lses = [], []
        for first in (True, False):
            qm = jnp.where(lo if first else ~lo, q, jnp.zeros_like(q))
            s = lax.dot_general(qm, k, _NT, preferred_element_type=F32)
            s = jnp.where(valid, s, NEG)
            m = jnp.max(s, axis=1, keepdims=True)
            p = jnp.exp(s - m)
            den = jnp.sum(p, axis=1, keepdims=True)
            pv = jnp.dot(p.astype(BF16), v, preferred_element_type=F32)
            outs.append(pv / den)
            lses.append(m + jnp.log(den))
        o = jnp.where(lo, outs[0], outs[1])
        lse = jnp.where(lo, lses[0], lses[1])
        if has_prev:
            lp = pl_ref[:, sl]
            mx = jnp.maximum(lp, lse)
            tot = mx + jnp.log(jnp.exp(lp - mx) + jnp.exp(lse - mx))
            o = po_ref[:, sl] * jnp.exp(lp - tot) + o * jnp.exp(lse - tot)
            lse = tot
        o_ref[:, sl] = o.astype(o_ref.dtype)
        if not final:
            lse_ref[:, sl] = lse


def dilated_branch(qkv, window, dilation, prev, final):
    bsz, seq, three_w = qkv.shape
    width = three_w // 3
    half = window // (2 * dilation)
    length = seq // dilation
    tu = _pick(length, (128, 64))
    assert tu % half == 0 and length % tu == 0
    per = tu // half
    n_half = length // half
    view = lambda a: a.reshape(bsz, length, dilation * a.shape[-1])

    def centre(col):
        return pl.BlockSpec((None, tu, width), lambda b, r, i: (b, i, 3 * r + col))

    def before(col):
        return pl.BlockSpec((None, half, width),
                            lambda b, r, i: (b, jnp.maximum(i * per - 1, 0), 3 * r + col))

    def after(col):
        return pl.BlockSpec((None, half, width),
                            lambda b, r, i: (b, jnp.minimum((i + 1) * per, n_half - 1), 3 * r + col))

    state_spec = pl.BlockSpec((None, tu, width), lambda b, r, i: (b, i, r))
    qv = view(qkv)
    in_specs = [centre(0), before(1), centre(1), after(1), before(2), centre(2), after(2)]
    args = [qv] * 7
    if prev is not None:
        in_specs += [state_spec, state_spec]
        args += [view(prev[0]), view(prev[1])]
    state_shape = (bsz, length, dilation * width)
    if final:
        out_shape = jax.ShapeDtypeStruct(state_shape, BF16)
        out_specs = state_spec
    else:
        out_shape = (jax.ShapeDtypeStruct(state_shape, F32), jax.ShapeDtypeStruct(state_shape, F32))
        out_specs = (state_spec, state_spec)
    out = pl.pallas_call(
        functools.partial(_dilated_kernel, has_prev=prev is not None, final=final,
                          half=half, length=length),
        grid=(bsz, dilation, length // tu),
        in_specs=in_specs,
        out_specs=out_specs,
        out_shape=out_shape,
        compiler_params=_cparams(("parallel", "parallel", "parallel")),
        name=f"dilated_d{dilation}",
    )(*args)
    unview = lambda a: a.reshape(bsz, seq, width)
    return unview(out) if final else (unview(out[0]), unview(out[1]))


def dilated_mixture(qkv):
    state = None
    for n, (window, dilation) in enumerate(DILATION_PAIRS):
        state = dilated_branch(qkv, window, dilation, state, final=n == len(DILATION_PAIRS) - 1)
    return state


def _diff_attn_kernel(q_ref, k_ref, v_ref, lq1_ref, lk1_ref, lq2_ref, lk2_ref, sub_ref, o_ref,
                      *, tk, lam_init):
    tq = q_ref.shape[0]
    seq = k_ref.shape[0]
    q = q_ref[...]
    lane = lax.broadcasted_iota(jnp.int32, (tq, LANES), 1)
    lo = lane < HEAD_DIM
    q1 = jnp.where(lo, q, jnp.zeros_like(q))
    q2 = jnp.where(lo, jnp.zeros_like(q), q)

    def update(qm, k, v, m, l, acc):
        s = lax.dot_general(qm, k, _NT, preferred_element_type=F32)
        m_new = jnp.maximum(m, jnp.max(s, axis=1, keepdims=True))
        alpha = jnp.exp(m - m_new)
        p = jnp.exp(s - m_new)
        l = alpha * l + jnp.sum(p, axis=1, keepdims=True)
        acc = alpha * acc + jnp.dot(p.astype(BF16), v, preferred_element_type=F32)
        return m_new, l, acc

    def body(kt, carry):
        m1, l1, a1, m2, l2, a2 = carry
        off = pl.multiple_of(kt * tk, tk)
        k = k_ref[pl.ds(off, tk), :]
        v = v_ref[pl.ds(off, tk), :]
        m1, l1, a1 = update(q1, k, v, m1, l1, a1)
        m2, l2, a2 = update(q2, k, v, m2, l2, a2)
        return m1, l1, a1, m2, l2, a2

    m0 = jnp.full((tq, 1), NEG, F32)
    l0 = jnp.zeros((tq, 1), F32)
    a0 = jnp.zeros((tq, LANES), F32)
    m1, l1, a1, m2, l2, a2 = lax.fori_loop(0, seq // tk, body, (m0, l0, a0, m0, l0, a0))

    lam = (jnp.exp(jnp.sum(lq1_ref[...] * lk1_ref[...], axis=1, keepdims=True))
           - jnp.exp(jnp.sum(lq2_ref[...] * lk2_ref[...], axis=1, keepdims=True)) + lam_init)
    o = a1 / l1 - lam * (a2 / l2)
    o_ref[...] = (_rms(o, sub_ref[...]) * (1.0 - lam_init)).astype(o_ref.dtype)


def diff_attention(qkv, lq1, lk1, lq2, lk2, subln, lam_init):
    bsz, seq, three_d = qkv.shape
    d = three_d // 3
    n_heads = d // LANES
    tq = _pick(seq, (256, 128))
    tk = _pick(seq, (512, 256, 128))
    row = lambda a: a.reshape(1, -1).astype(F32)
    small = lambda n: pl.BlockSpec((1, n), lambda b, h, i: (0, 0))
    return pl.pallas_call(
        functools.partial(_diff_attn_kernel, tk=tk, lam_init=lam_init),
        grid=(bsz, n_heads, seq // tq),
        in_specs=[pl.BlockSpec((None, tq, LANES), lambda b, h, i: (b, i, h)),
                  pl.BlockSpec((None, seq, LANES), lambda b, h, i: (b, 0, n_heads + h)),
                  pl.BlockSpec((None, seq, LANES), lambda b, h, i: (b, 0, 2 * n_heads + h)),
                  small(HEAD_DIM), small(HEAD_DIM), small(HEAD_DIM), small(HEAD_DIM),
                  small(LANES)],
        out_specs=pl.BlockSpec((None, tq, LANES), lambda b, h, i: (b, i, h)),
        out_shape=jax.ShapeDtypeStruct((bsz, seq, d), BF16),
        compiler_params=_cparams(("parallel", "parallel", "parallel")),
        name="diff_attention",
    )(qkv, qkv, qkv, row(lq1), row(lk1), row(lq2), row(lk2), row(subln))


def _out_proj_kernel(*refs, n_in):
    ys = refs[:n_in]
    ws = refs[n_in:2 * n_in]
    x_ref, g_ref, o_ref = refs[2 * n_in:]
    m = jnp.dot(ys[0][...], ws[0][...], preferred_element_type=F32)
    for y_ref, w_ref in zip(ys[1:], ws[1:]):
        m = m + jnp.dot(y_ref[...], w_ref[...], preferred_element_type=F32)
    o_ref[...] = x_ref[...] + _rms(m, g_ref[...])


def out_proj_residual(ys, ws, x, g):
    t, d = x.shape
    tm = _pick(t, (512, 256, 128))
    in_specs = ([pl.BlockSpec((tm, y.shape[1]), lambda i: (i, 0)) for y in ys]
                + [pl.BlockSpec(w.shape, lambda i: (0, 0)) for w in ws]
                + [pl.BlockSpec((tm, d), lambda i: (i, 0)), pl.BlockSpec((1, d), lambda i: (0, 0))])
    return pl.pallas_call(
        functools.partial(_out_proj_kernel, n_in=len(ys)),
        grid=(t // tm,),
        in_specs=in_specs,
        out_specs=pl.BlockSpec((tm, d), lambda i: (i, 0)),
        out_shape=jax.ShapeDtypeStruct((t, d), F32),
        compiler_params=_cparams(("parallel",)),
        name="out_proj",
    )(*ys, *ws, x, g)


def _ffn_kernel(x_ref, gpre_ref, gpost_ref, wg_ref, wu_ref, wd_ref, o_ref, h_ref, acc_ref):
    f = pl.program_id(1)

    @pl.when(f == 0)
    def _():
        h_ref[...] = _rms(x_ref[...], gpre_ref[...]).astype(BF16)
        acc_ref[...] = jnp.zeros_like(acc_ref)

    h = h_ref[...]
    gate = jnp.dot(h, wg_ref[...], preferred_element_type=F32)
    up = jnp.dot(h, wu_ref[...], preferred_element_type=F32)
    act = (gate / (1.0 + jnp.exp(-gate))) * up
    acc_ref[...] += jnp.dot(act.astype(BF16), wd_ref[...], preferred_element_type=F32)

    @pl.when(f == pl.num_programs(1) - 1)
    def _():
        o_ref[...] = x_ref[...] + _rms(acc_ref[...], gpost_ref[...])


def ffn_residual(x, g_pre, g_post, w_gate, w_up, w_down):
    t, d = x.shape
    f = w_gate.shape[1]
    tm = _pick(t, (512, 256, 128))
    tf = _pick(f, (512, 256, 128))
    return pl.pallas_call(
        _ffn_kernel,
        grid=(t // tm, f // tf),
        in_specs=[pl.BlockSpec((tm, d), lambda i, j: (i, 0)),
                  pl.BlockSpec((1, d), lambda i, j: (0, 0)),
                  pl.BlockSpec((1, d), lambda i, j: (0, 0)),
                  pl.BlockSpec((d, tf), lambda i, j: (0, j)),
                  pl.BlockSpec((d, tf), lambda i, j: (0, j)),
                  pl.BlockSpec((tf, d), lambda i, j: (j, 0))],
        out_specs=pl.BlockSpec((tm, d), lambda i, j: (i, 0)),
        out_shape=jax.ShapeDtypeStruct((t, d), F32),
        scratch_shapes=[pltpu.VMEM((tm, d), BF16), pltpu.VMEM((tm, d), F32)],
        compiler_params=_cparams(("parallel", "arbitrary")),
        name="ffn",
    )(x, g_pre, g_post, w_gate, w_up, w_down)


def kernel(x, norm_mix_pre, norm_mix_post, norm_ffn_pre, norm_ffn_post, ev_w_in, ev_conv_w, ev_conv_b, ev_w_r, ev_b_r, ev_w_i, ev_b_i, ev_lam, ev_w_out, od_w_in, od_lam_q1, od_lam_k1, od_lam_q2, od_lam_k2, od_subln, od_w_out, ffn_w_gate, ffn_w_up, ffn_w_down):
    bsz, seq, d = x.shape
    depth = norm_mix_pre.shape[0]
    t = bsz * seq
    aw = ev_conv_w.shape[-1]
    rope = rope_tables(seq)
    gain = lambda g: g.reshape(1, d).astype(F32)
    xf = x.reshape(t, d)
    for layer in range(depth):
        j = layer // 2
        g_pre = gain(norm_mix_pre[layer])
        if layer % 2 == 0:
            w_in = ev_w_in[j].astype(BF16)
            z_a = norm_matmul(xf, g_pre, w_in[:, :2 * aw], F32).reshape(bsz, seq, 2 * aw)
            qkv = qkv_proj(xf, g_pre, w_in[:, 2 * aw:], rope, seq)
            rows = lambda p: p.reshape(1, aw).astype(F32)
            h_fwd = None
            for direction in (0, 1):
                h_fwd = rglru_direction(
                    z_a, ev_conv_w[j].astype(F32), rows(ev_conv_b[j]),
                    gate_group_weights(ev_w_r[j, direction], ev_w_i[j, direction]),
                    rows(ev_b_r[j, direction]), rows(ev_b_i[j, direction]),
                    rows(ev_lam[j, direction]), h_fwd, reverse=direction == 1)
            y_a = h_fwd.reshape(t, aw)
            y_b = dilated_mixture(qkv.reshape(bsz, seq, -1)).reshape(t, -1)
            w_out = ev_w_out[j].astype(BF16)
            ys, ws = [y_a, y_b], [w_out[:aw], w_out[aw:]]
        else:
            lam_init = 0.8 - 0.6 * math.exp(-0.3 * layer)
            qkv = qkv_proj(xf, g_pre, od_w_in[j].astype(BF16), rope, seq)
            y = diff_attention(qkv.reshape(bsz, seq, -1), od_lam_q1[j], od_lam_k1[j],
                               od_lam_q2[j], od_lam_k2[j], od_subln[j], lam_init)
            ys, ws = [y.reshape(t, -1)], [od_w_out[j].astype(BF16)]
        xf = out_proj_residual(ys, ws, xf, gain(norm_mix_post[layer]))
        xf = ffn_residual(xf, gain(norm_ffn_pre[layer]), gain(norm_ffn_post[layer]),
                          ffn_w_gate[layer].astype(BF16), ffn_w_up[layer].astype(BF16),
                          ffn_w_down[layer].astype(BF16))
    return xf.reshape(bsz, seq, d)
```

```python
import functools
import math

import jax
import jax.numpy as jnp
from jax import lax
from jax.experimental import pallas as pl
from jax.experimental.pallas import tpu as pltpu

F32 = jnp.float32
BF16 = jnp.bfloat16

EPS = 1e-6
NEG = -1e30
HEAD_DIM = 64
ROT_DIM = HEAD_DIM // 4
ROT_HALF = ROT_DIM // 2
ROPE_THETA = 500000.0
CONV_WIDTH = 4
RG_C = 8.0
DILATION_PAIRS = ((128, 1), (512, 4), (2048, 16))
GELU_C = math.sqrt(2.0 / math.pi)

LANES = 128
SUBLANES = 8
MXU_DIM = 256
VMEM_LIMIT_BYTES = 56 * 1024 * 1024

DIFF_TK = 256
DIFF_UNROLL = 4
BF16_SUBLANES = 16
ONES_ROWS = BF16_SUBLANES

LOG2E = math.log2(math.e)
Q_SCALE = HEAD_DIM ** -0.5 * LOG2E

_NT = (((1,), (1,)), ((), ()))


def _pick(n, prefs):
    for p in prefs:
        if n % p == 0:
            return p
    return n


def _cparams(sem):
    return pltpu.CompilerParams(dimension_semantics=sem, vmem_limit_bytes=VMEM_LIMIT_BYTES)


def _rms(x, g):
    return x * lax.rsqrt(jnp.mean(x * x, axis=-1, keepdims=True) + EPS) * g


def _norm_matmul_kernel(x_ref, g_ref, w_ref, o_ref, h_ref):
    @pl.when(pl.program_id(1) == 0)
    def _():
        h_ref[...] = _rms(x_ref[...], g_ref[...]).astype(BF16)

    o_ref[...] = jnp.dot(h_ref[...], w_ref[...], preferred_element_type=F32).astype(o_ref.dtype)


def norm_matmul(x, g, w, out_dtype):
    t, d = x.shape
    n = w.shape[1]
    tm = _pick(t, (512, 256, 128))
    tn = _pick(n, (1024, 512, 256, 128))
    return pl.pallas_call(
        _norm_matmul_kernel,
        grid=(t // tm, n // tn),
        in_specs=[pl.BlockSpec((tm, d), lambda i, j: (i, 0)),
                  pl.BlockSpec((1, d), lambda i, j: (0, 0)),
                  pl.BlockSpec((d, tn), lambda i, j: (0, j))],
        out_specs=pl.BlockSpec((tm, tn), lambda i, j: (i, j)),
        out_shape=jax.ShapeDtypeStruct((t, n), out_dtype),
        scratch_shapes=[pltpu.VMEM((tm, d), BF16)],
        compiler_params=_cparams(("parallel", "arbitrary")),
        name="norm_matmul",
    )(x, g, w)


def _qkv_kernel(x_ref, g_ref, w_ref, c_ref, sa_ref, sb_ref, o_ref, h_ref, *, n_q, n_rope):
    j = pl.program_id(1)

    @pl.when(j == 0)
    def _():
        h_ref[...] = _rms(x_ref[...], g_ref[...]).astype(BF16)

    acc = jnp.dot(h_ref[...], w_ref[...], preferred_element_type=F32)
    tn = acc.shape[1]

    @pl.when(j < n_rope)
    def _():
        scale = jnp.where(j < n_q, Q_SCALE, 1.0).astype(F32)
        c = c_ref[...] * scale
        sa = sa_ref[...] * scale
        sb = sb_ref[...] * scale
        for cb in range(tn // LANES):
            a = acc[:, cb * LANES:(cb + 1) * LANES]
            r = (a * c + pltpu.roll(a, LANES - ROT_HALF, 1) * sa
                 + pltpu.roll(a, ROT_HALF, 1) * sb)
            o_ref[:, cb * LANES:(cb + 1) * LANES] = r.astype(o_ref.dtype)

    @pl.when(j >= n_rope)
    def _():
        o_ref[...] = acc.astype(o_ref.dtype)


def qkv_proj(x, g, w, rope, seq, n_parts):
    t, d = x.shape
    n = w.shape[1]
    width = n // n_parts
    tm = _pick(seq, (512, 256, 128))
    tn = _pick(width, (1024, 512, 256, 128))
    n_q = width // tn
    c, sa, sb = rope
    nblk_s = seq // tm
    tab_spec = pl.BlockSpec((tm, LANES), lambda i, j: (i % nblk_s, 0))
    return pl.pallas_call(
        functools.partial(_qkv_kernel, n_q=n_q, n_rope=2 * n_q),
        grid=(t // tm, n // tn),
        in_specs=[pl.BlockSpec((tm, d), lambda i, j: (i, 0)),
                  pl.BlockSpec((1, d), lambda i, j: (0, 0)),
                  pl.BlockSpec((d, tn), lambda i, j: (0, j)),
                  tab_spec, tab_spec, tab_spec],
        out_specs=pl.BlockSpec((tm, tn), lambda i, j: (i, j)),
        out_shape=jax.ShapeDtypeStruct((t, n), BF16),
        scratch_shapes=[pltpu.VMEM((tm, d), BF16)],
        compiler_params=_cparams(("parallel", "arbitrary")),
        name="qkv_proj",
    )(x, g, w, c, sa, sb)


def rope_tables(seq):
    pos = jnp.arange(seq, dtype=F32)
    inv = ROPE_THETA ** (-jnp.arange(0, ROT_DIM, 2, dtype=F32) / ROT_DIM)
    ang = pos[:, None] * inv[None, :]
    cos, sin = jnp.cos(ang), jnp.sin(ang)
    zeros = jnp.zeros((seq, HEAD_DIM - ROT_DIM), F32)
    zh = jnp.zeros((seq, ROT_HALF), F32)
    c = jnp.concatenate([cos, cos, zeros + 1.0], axis=1)
    sa = jnp.concatenate([-sin, zh, zeros], axis=1)
    sb = jnp.concatenate([zh, sin, zeros], axis=1)
    rep = LANES // HEAD_DIM
    return tuple(jnp.tile(tab, (1, rep)) for tab in (c, sa, sb))


def _shift_rows(x, s, fill, reverse):
    n = x.shape[0]
    if s % SUBLANES == 0:
        pad = jnp.full((s, x.shape[1]), fill, x.dtype)
        return (jnp.concatenate([x[s:], pad], axis=0) if reverse
                else jnp.concatenate([pad, x[:n - s]], axis=0))
    row = lax.broadcasted_iota(jnp.int32, x.shape, 0)
    if reverse:
        return jnp.where(row < n - s, pltpu.roll(x, n - s, 0), fill)
    return jnp.where(row >= s, pltpu.roll(x, s, 0), fill)


def _rglru_kernel(*refs, reverse, final, n_blk):
    if final:
        (xp_ref, xc_ref, xn_ref, cw_ref, cb_ref, wg_ref, br_ref, bi_ref, lam_ref,
         gate_ref, prev_ref, o_ref, carry_ref) = refs
    else:
        (xp_ref, xc_ref, xn_ref, cw_ref, cb_ref, wg_ref, br_ref, bi_ref, lam_ref,
         o_ref, carry_ref) = refs
    i = pl.program_id(1)
    blk = n_blk - 1 - i if reverse else i
    tb, aw = xc_ref.shape

    @pl.when(i == 0)
    def _():
        carry_ref[...] = jnp.zeros_like(carry_ref)

    x0 = xc_ref[...]
    xp = jnp.where(blk > 0, xp_ref[...], 0.0)
    xn = jnp.where(blk < n_blk - 1, xn_ref[...], 0.0)
    row = lax.broadcasted_iota(jnp.int32, (tb, aw), 0)
    zpad = jnp.zeros((tb - SUBLANES, aw), F32)
    conv = cb_ref[...] + cw_ref[2:3, :] * x0
    for back in (1, 2):
        head = jnp.concatenate([pltpu.roll(xp, back, 0), zpad], axis=0)
        shifted = jnp.where(row < back, head, pltpu.roll(x0, back, 0))
        conv = conv + cw_ref[2 - back:3 - back, :] * shifted
    tail = jnp.concatenate([zpad, pltpu.roll(xn, SUBLANES - 1, 0)], axis=0)
    shifted = jnp.where(row == tb - 1, tail, pltpu.roll(x0, tb - 1, 0))
    conv = conv + cw_ref[3:4, :] * shifted

    n_grp, gw, _ = wg_ref.shape
    cbf = conv.astype(BF16)
    zr, zi = [], []
    for g in range(n_grp):
        z = jnp.dot(cbf[:, g * gw:(g + 1) * gw], wg_ref[g], preferred_element_type=F32)
        zr.append(z[:, :gw])
        zi.append(z[:, gw:])
    zr = jnp.concatenate(zr, axis=1) if n_grp > 1 else zr[0]
    zi = jnp.concatenate(zi, axis=1) if n_grp > 1 else zi[0]
    r = 1.0 / (1.0 + jnp.exp(-(zr + br_ref[...])))
    ig = 1.0 / (1.0 + jnp.exp(-(zi + bi_ref[...])))
    nl = -lam_ref[...]
    softplus = jnp.maximum(nl, 0.0) + jnp.log(1.0 + jnp.exp(-jnp.abs(nl)))
    a = jnp.exp(-RG_C * r * softplus)
    u = jnp.sqrt(jnp.maximum(1.0 - a * a, 0.0)) * (ig * conv)

    s = 1
    while s < tb:
        a_sh = _shift_rows(a, s, 1.0, reverse)
        u_sh = _shift_rows(u, s, 0.0, reverse)
        u = a * u_sh + u
        a = a * a_sh
        s *= 2
    edge = 0 if reverse else SUBLANES - 1
    h = a * carry_ref[edge:edge + 1, :] + u
    carry_ref[...] = h[:SUBLANES] if reverse else h[tb - SUBLANES:]

    if final:
        gt = gate_ref[...]
        gelu = 0.5 * gt * (1.0 + jnp.tanh(GELU_C * (gt + 0.044715 * (gt * gt * gt))))
        o_ref[...] = ((prev_ref[...] + h) * gelu).astype(o_ref.dtype)
    else:
        o_ref[...] = h


def rglru_direction(z_a, conv_w, conv_b, w_gate, b_r, b_i, lam, prev, reverse):
    bsz, seq, two_aw = z_a.shape
    aw = two_aw // 2
    tb = _pick(seq, (256, 128, 64, 32, 16, 8))
    n_blk = seq // tb
    rows8 = seq // SUBLANES
    per8 = tb // SUBLANES
    final = prev is not None

    def blk(i):
        return n_blk - 1 - i if reverse else i

    xr_specs = [
        pl.BlockSpec((None, SUBLANES, aw), lambda b, i: (b, jnp.maximum(blk(i) * per8 - 1, 0), 1)),
        pl.BlockSpec((None, tb, aw), lambda b, i: (b, blk(i), 1)),
        pl.BlockSpec((None, SUBLANES, aw),
                     lambda b, i: (b, jnp.minimum((blk(i) + 1) * per8, rows8 - 1), 1)),
    ]
    full = lambda a: pl.BlockSpec(a.shape, lambda b, i: (0,) * a.ndim)
    params = [conv_w, conv_b, w_gate, b_r, b_i, lam]
    in_specs = xr_specs + [full(p) for p in params]
    args = [z_a, z_a, z_a] + params
    if final:
        in_specs += [pl.BlockSpec((None, tb, aw), lambda b, i: (b, blk(i), 0)),
                     pl.BlockSpec((None, tb, aw), lambda b, i: (b, blk(i), 0))]
        args += [z_a, prev]
    return pl.pallas_call(
        functools.partial(_rglru_kernel, reverse=reverse, final=final, n_blk=n_blk),
        grid=(bsz, n_blk),
        in_specs=in_specs,
        out_specs=pl.BlockSpec((None, tb, aw), lambda b, i: (b, blk(i), 0)),
        out_shape=jax.ShapeDtypeStruct((bsz, seq, aw), BF16 if final else F32),
        scratch_shapes=[pltpu.VMEM((SUBLANES, aw), F32)],
        compiler_params=_cparams(("parallel", "arbitrary")),
        name="rglru_bwd" if reverse else "rglru_fwd",
    )(*args)


def gate_group_weights(w_r, w_i):
    n, bs, _ = w_r.shape
    aw = n * bs
    gw = _pick(aw, (MXU_DIM, LANES))
    per = gw // bs
    n_grp = aw // gw
    eye = jnp.eye(per, dtype=w_r.dtype)

    def dense(w):
        wg = w.reshape(n_grp, per, bs, bs)
        return jnp.einsum('gaef,ab->gaebf', wg, eye).reshape(n_grp, gw, gw)

    return jnp.concatenate([dense(w_r), dense(w_i)], axis=2).astype(BF16)


def _dilated_kernel(*refs, has_prev, final, half, length):
    q_ref, kp_ref, kc_ref, kn_ref, vp_ref, vc_ref, vn_ref = refs[:7]
    rest = refs[7:]
    if has_prev:
        po_ref, pl_ref = rest[:2]
        rest = rest[2:]
    o_ref = rest[0]
    lse_ref = None if final else rest[1]
    i = pl.program_id(2)
    tu, width = q_ref.shape
    nk = tu + 2 * half

    rowq = lax.broadcasted_iota(jnp.int32, (tu, nk), 0)
    colk = lax.broadcasted_iota(jnp.int32, (tu, nk), 1)
    rel = colk - half - rowq
    key_u = i * tu - half + colk
    valid = (jnp.abs(rel) <= half) & (key_u >= 0) & (key_u < length)
    lane = lax.broadcasted_iota(jnp.int32, (tu, LANES), 1)
    lo = lane < HEAD_DIM

    for hp in range(width // LANES):
        sl = slice(hp * LANES, (hp + 1) * LANES)
        q = q_ref[:, sl]
        k = jnp.concatenate([kp_ref[:, sl], kc_ref[:, sl], kn_ref[:, sl]], axis=0)
        v = jnp.concatenate([vp_ref[:, sl], vc_ref[:, sl], vn_ref[:, sl]], axis=0)
        outs, lses = [], []
        for first in (True, False):
            qm = jnp.where(lo if first else ~lo, q, jnp.zeros_like(q))
            s = lax.dot_general(qm, k, _NT, preferred_element_type=F32)
            s = jnp.where(valid, s, NEG)
            m = jnp.max(s, axis=1, keepdims=True)
            p = jnp.exp2(s - m)
            den = jnp.sum(p, axis=1, keepdims=True)
            pv = jnp.dot(p.astype(BF16), v, preferred_element_type=F32)
            outs.append(pv / den)
            lses.append(m + jnp.log2(den))
        o = jnp.where(lo, outs[0], outs[1])
        lse = jnp.where(lo, lses[0], lses[1])
        if has_prev:
            lp = pl_ref[:, sl]
            mx = jnp.maximum(lp, lse)
            tot = mx + jnp.log2(jnp.exp2(lp - mx) + jnp.exp2(lse - mx))
            o = po_ref[:, sl] * jnp.exp2(lp - tot) + o * jnp.exp2(lse - tot)
            lse = tot
        o_ref[:, sl] = o.astype(o_ref.dtype)
        if not final:
            lse_ref[:, sl] = lse


def dilated_branch(qkv, window, dilation, prev, final):
    bsz, seq, three_w = qkv.shape
    width = three_w // 3
    half = window // (2 * dilation)
    length = seq // dilation
    tu = _pick(length, (128, 64))
    assert tu % half == 0 and length % tu == 0
    per = tu // half
    n_half = length // half
    view = lambda a: a.reshape(bsz, length, dilation * a.shape[-1])

    def centre(col):
        return pl.BlockSpec((None, tu, width), lambda b, r, i: (b, i, 3 * r + col))

    def before(col):
        return pl.BlockSpec((None, half, width),
                            lambda b, r, i: (b, jnp.maximum(i * per - 1, 0), 3 * r + col))

    def after(col):
        return pl.BlockSpec((None, half, width),
                            lambda b, r, i: (b, jnp.minimum((i + 1) * per, n_half - 1), 3 * r + col))

    state_spec = pl.BlockSpec((None, tu, width), lambda b, r, i: (b, i, r))
    qv = view(qkv)
    in_specs = [centre(0), before(1), centre(1), after(1), before(2), centre(2), after(2)]
    args = [qv] * 7
    if prev is not None:
        in_specs += [state_spec, state_spec]
        args += [view(prev[0]), view(prev[1])]
    state_shape = (bsz, length, dilation * width)
    if final:
        out_shape = jax.ShapeDtypeStruct(state_shape, BF16)
        out_specs = state_spec
    else:
        out_shape = (jax.ShapeDtypeStruct(state_shape, F32), jax.ShapeDtypeStruct(state_shape, F32))
        out_specs = (state_spec, state_spec)
    out = pl.pallas_call(
        functools.partial(_dilated_kernel, has_prev=prev is not None, final=final,
                          half=half, length=length),
        grid=(bsz, dilation, length // tu),
        in_specs=in_specs,
        out_specs=out_specs,
        out_shape=out_shape,
        compiler_params=_cparams(("parallel", "parallel", "parallel")),
        name=f"dilated_d{dilation}",
    )(*args)
    unview = lambda a: a.reshape(bsz, seq, width)
    return unview(out) if final else (unview(out[0]), unview(out[1]))


def dilated_mixture(qkv):
    state = None
    for n, (window, dilation) in enumerate(DILATION_PAIRS):
        state = dilated_branch(qkv, window, dilation, state, final=n == len(DILATION_PAIRS) - 1)
    return state


def _norm_matmul_t_kernel(x_ref, g_ref, w_ref, o_ref, h_ref):
    @pl.when(pl.program_id(1) == 0)
    def _():
        h_ref[...] = _rms(x_ref[...], g_ref[...]).astype(BF16)

    acc_t = jnp.dot(h_ref[...], w_ref[...], preferred_element_type=F32).T
    n_chunk, _, tk = o_ref.shape
    for c in range(n_chunk):
        o_ref[c] = acc_t[:, c * tk:(c + 1) * tk].astype(o_ref.dtype)


def norm_matmul_t(x, g, w, bsz, seq, tk):
    t, d = x.shape
    n = w.shape[1]
    tm = _pick(seq, (512, 256, 128))
    tn = _pick(n, (512, 256, 128))
    assert tm % tk == 0
    per = tm // tk
    nblk_s = seq // tm
    return pl.pallas_call(
        _norm_matmul_t_kernel,
        grid=(t // tm, n // tn),
        in_specs=[pl.BlockSpec((tm, d), lambda i, j: (i, 0)),
                  pl.BlockSpec((1, d), lambda i, j: (0, 0)),
                  pl.BlockSpec((d, tn), lambda i, j: (0, j))],
        out_specs=pl.BlockSpec((None, per, tn, tk), lambda i, j: (i // nblk_s, i % nblk_s, j, 0)),
        out_shape=jax.ShapeDtypeStruct((bsz, seq // tk, n, tk), BF16),
        scratch_shapes=[pltpu.VMEM((tm, d), BF16)],
        compiler_params=_cparams(("parallel", "arbitrary")),
        name="norm_matmul_t",
    )(x, g, w)


def _diff_attn_kernel(q_ref, k_ref, vt_ref, lq1_ref, lk1_ref, lq2_ref, lk2_ref, sub_ref, o_ref,
                      sa_ref, sb_ref, acc_ref, *, lam_init):
    tq = q_ref.shape[0]
    n_kt, _, tk = vt_ref.shape
    q = q_ref[...]
    lane = lax.broadcasted_iota(jnp.int32, (tq, LANES), 1)
    lo = lane < HEAD_DIM
    q1 = jnp.where(lo, q, jnp.zeros_like(q))
    q2 = jnp.where(lo, jnp.zeros_like(q), q)

    def scores(kt, s_ref):
        k = k_ref[pl.ds(pl.multiple_of(kt * tk, tk), tk), :]
        s_ref[0] = lax.dot_general(k, q1, _NT, preferred_element_type=F32)
        s_ref[1] = lax.dot_general(k, q2, _NT, preferred_element_type=F32)

    ones = jnp.ones((ONES_ROWS, tk), BF16)

    def absorb(kt, s_ref, ms):
        vt = jnp.concatenate([vt_ref[kt], ones], axis=0)
        out = []
        for j in range(2):
            st = s_ref[j]
            m_new = jnp.maximum(ms[j], jnp.max(st, axis=0, keepdims=True))
            alpha = jnp.exp2(ms[j] - m_new)
            p = jnp.exp2((st - m_new).astype(BF16))
            acc_ref[j] = alpha * acc_ref[j] + jnp.dot(vt, p, preferred_element_type=F32)
            out.append(m_new)
        return tuple(out)

    def body(it, ms):
        kt = DIFF_UNROLL * it
        bufs = (sa_ref, sb_ref)
        for u in range(DIFF_UNROLL):
            scores(jnp.minimum(kt + u + 1, n_kt - 1), bufs[(u + 1) % 2])
            ms = absorb(kt + u, bufs[u % 2], ms)
        return ms

    m0 = jnp.full((1, tq), NEG, F32)
    acc_ref[...] = jnp.zeros_like(acc_ref)
    scores(0, sa_ref)
    lax.fori_loop(0, n_kt // DIFF_UNROLL, body, (m0, m0))
    a1, l1 = acc_ref[0, :LANES], acc_ref[0, LANES:LANES + 1]
    a2, l2 = acc_ref[1, :LANES], acc_ref[1, LANES:LANES + 1]

    lam = (jnp.exp(jnp.sum(lq1_ref[...] * lk1_ref[...], axis=1, keepdims=True))
           - jnp.exp(jnp.sum(lq2_ref[...] * lk2_ref[...], axis=1, keepdims=True)) + lam_init)
    o_t = a1 * (1.0 / l1) - lam * (a2 * (1.0 / l2))
    o_t = o_t * (lax.rsqrt(jnp.mean(o_t * o_t, axis=0, keepdims=True) + EPS) * (1.0 - lam_init))
    o_ref[...] = (o_t.T * sub_ref[...]).astype(o_ref.dtype)


def diff_attention(qk, vt, lq1, lk1, lq2, lk2, subln, lam_init):
    bsz, seq, two_d = qk.shape
    d = two_d // 2
    n_heads = d // LANES
    n_kt, tk = vt.shape[1], vt.shape[3]
    assert DIFF_UNROLL % 2 == 0 and n_kt % DIFF_UNROLL == 0
    tq = _pick(seq, (512, 256, 128))
    row = lambda a: a.reshape(1, -1).astype(F32)
    small = lambda n: pl.BlockSpec((1, n), lambda b, h, i: (0, 0))
    return pl.pallas_call(
        functools.partial(_diff_attn_kernel, lam_init=lam_init),
        grid=(bsz, n_heads, seq // tq),
        in_specs=[pl.BlockSpec((None, tq, LANES), lambda b, h, i: (b, i, h)),
                  pl.BlockSpec((None, seq, LANES), lambda b, h, i: (b, 0, n_heads + h)),
                  pl.BlockSpec((None, n_kt, LANES, tk), lambda b, h, i: (b, 0, h, 0)),
                  small(HEAD_DIM), small(HEAD_DIM), small(HEAD_DIM), small(HEAD_DIM),
                  small(LANES)],
        out_specs=pl.BlockSpec((None, tq, LANES), lambda b, h, i: (b, i, h)),
        out_shape=jax.ShapeDtypeStruct((bsz, seq, d), BF16),
        scratch_shapes=[pltpu.VMEM((2, tk, tq), F32), pltpu.VMEM((2, tk, tq), F32),
                        pltpu.VMEM((2, LANES + ONES_ROWS, tq), F32)],
        compiler_params=_cparams(("parallel", "parallel", "parallel")),
        name="diff_attention",
    )(qk, qk, vt, row(lq1), row(lk1), row(lq2), row(lk2), row(subln))


def _out_proj_kernel(*refs, n_in):
    ys = refs[:n_in]
    ws = refs[n_in:2 * n_in]
    x_ref, g_ref, o_ref = refs[2 * n_in:]
    m = jnp.dot(ys[0][...], ws[0][...], preferred_element_type=F32)
    for y_ref, w_ref in zip(ys[1:], ws[1:]):
        m = m + jnp.dot(y_ref[...], w_ref[...], preferred_element_type=F32)
    o_ref[...] = x_ref[...] + _rms(m, g_ref[...])


def out_proj_residual(ys, ws, x, g):
    t, d = x.shape
    tm = _pick(t, (512, 256, 128))
    in_specs = ([pl.BlockSpec((tm, y.shape[1]), lambda i: (i, 0)) for y in ys]
                + [pl.BlockSpec(w.shape, lambda i: (0, 0)) for w in ws]
                + [pl.BlockSpec((tm, d), lambda i: (i, 0)), pl.BlockSpec((1, d), lambda i: (0, 0))])
    return pl.pallas_call(
        functools.partial(_out_proj_kernel, n_in=len(ys)),
        grid=(t // tm,),
        in_specs=in_specs,
        out_specs=pl.BlockSpec((tm, d), lambda i: (i, 0)),
        out_shape=jax.ShapeDtypeStruct((t, d), F32),
        compiler_params=_cparams(("parallel",)),
        name="out_proj",
    )(*ys, *ws, x, g)


def _ffn_kernel(x_ref, gpre_ref, gpost_ref, wg_ref, wu_ref, wd_ref, o_ref, h_ref, acc_ref):
    f = pl.program_id(1)

    @pl.when(f == 0)
    def _():
        h_ref[...] = _rms(x_ref[...], gpre_ref[...]).astype(BF16)
        acc_ref[...] = jnp.zeros_like(acc_ref)

    h = h_ref[...]
    gate = jnp.dot(h, wg_ref[...], preferred_element_type=F32)
    up = jnp.dot(h, wu_ref[...], preferred_element_type=F32)
    act = (gate / (1.0 + jnp.exp(-gate))) * up
    acc_ref[...] += jnp.dot(act.astype(BF16), wd_ref[...], preferred_element_type=F32)

    @pl.when(f == pl.num_programs(1) - 1)
    def _():
        o_ref[...] = x_ref[...] + _rms(acc_ref[...], gpost_ref[...])


def ffn_residual(x, g_pre, g_post, w_gate, w_up, w_down):
    t, d = x.shape
    f = w_gate.shape[1]
    tm = _pick(t, (512, 256, 128))
    tf = _pick(f, (512, 256, 128))
    return pl.pallas_call(
        _ffn_kernel,
        grid=(t // tm, f // tf),
        in_specs=[pl.BlockSpec((tm, d), lambda i, j: (i, 0)),
                  pl.BlockSpec((1, d), lambda i, j: (0, 0)),
                  pl.BlockSpec((1, d), lambda i, j: (0, 0)),
                  pl.BlockSpec((d, tf), lambda i, j: (0, j)),
                  pl.BlockSpec((d, tf), lambda i, j: (0, j)),
                  pl.BlockSpec((tf, d), lambda i, j: (j, 0))],
        out_specs=pl.BlockSpec((tm, d), lambda i, j: (i, 0)),
        out_shape=jax.ShapeDtypeStruct((t, d), F32),
        scratch_shapes=[pltpu.VMEM((tm, d), BF16), pltpu.VMEM((tm, d), F32)],
        compiler_params=_cparams(("parallel", "arbitrary")),
        name="ffn",
    )(x, g_pre, g_post, w_gate, w_up, w_down)


def kernel(x, norm_mix_pre, norm_mix_post, norm_ffn_pre, norm_ffn_post, ev_w_in, ev_conv_w, ev_conv_b, ev_w_r, ev_b_r, ev_w_i, ev_b_i, ev_lam, ev_w_out, od_w_in, od_lam_q1, od_lam_k1, od_lam_q2, od_lam_k2, od_subln, od_w_out, ffn_w_gate, ffn_w_up, ffn_w_down):
    bsz, seq, d = x.shape
    depth = norm_mix_pre.shape[0]
    t = bsz * seq
    aw = ev_conv_w.shape[-1]
    rope = rope_tables(seq)
    gain = lambda g: g.reshape(1, d).astype(F32)
    xf = x.reshape(t, d)
    for layer in range(depth):
        j = layer // 2
        g_pre = gain(norm_mix_pre[layer])
        if layer % 2 == 0:
            w_in = ev_w_in[j].astype(BF16)
            z_a = norm_matmul(xf, g_pre, w_in[:, :2 * aw], F32).reshape(bsz, seq, 2 * aw)
            qkv = qkv_proj(xf, g_pre, w_in[:, 2 * aw:], rope, seq, 3)
            rows =lambda p: p.reshape(1, aw).astype(F32)
            h_fwd = None
            for direction in (0, 1):
                h_fwd = rglru_direction(
                    z_a, ev_conv_w[j].astype(F32), rows(ev_conv_b[j]),
                    gate_group_weights(ev_w_r[j, direction], ev_w_i[j, direction]),
                    rows(ev_b_r[j, direction]), rows(ev_b_i[j, direction]),
                    rows(ev_lam[j, direction]), h_fwd, reverse=direction == 1)
            y_a = h_fwd.reshape(t, aw)
            y_b = dilated_mixture(qkv.reshape(bsz, seq, -1)).reshape(t, -1)
            w_out = ev_w_out[j].astype(BF16)
            ys, ws = [y_a, y_b], [w_out[:aw], w_out[aw:]]
        else:
            lam_init = 0.8 - 0.6 * math.exp(-0.3 * layer)
            w_in = od_w_in[j].astype(BF16)
            qk = qkv_proj(xf, g_pre, w_in[:, :2 * d], rope, seq, 2)
            vt = norm_matmul_t(xf, g_pre, w_in[:, 2 * d:], bsz, seq, _pick(seq, (DIFF_TK, LANES)))
            y = diff_attention(qk.reshape(bsz, seq, -1), vt, od_lam_q1[j], od_lam_k1[j],
                               od_lam_q2[j], od_lam_k2[j], od_subln[j], lam_init)
            ys, ws = [y.reshape(t, -1)], [od_w_out[j].astype(BF16)]
        xf = out_proj_residual(ys, ws, xf, gain(norm_mix_post[layer]))
        xf = ffn_residual(xf, gain(norm_ffn_pre[layer]), gain(norm_ffn_post[layer]),
                          ffn_w_gate[layer].astype(BF16), ffn_w_up[layer].astype(BF16),
                          ffn_w_down[layer].astype(BF16))
    return xf.reshape(bsz, seq, d)
```

```python
import functools
import math

import jax
import jax.numpy as jnp
from jax import lax
from jax.experimental import pallas as pl
from jax.experimental.pallas import tpu as pltpu

F32 = jnp.float32
BF16 = jnp.bfloat16

EPS = 1e-6
NEG = -1e30
HEAD_DIM = 64
ROT_DIM = HEAD_DIM // 4
ROT_HALF = ROT_DIM // 2
ROPE_THETA = 500000.0
CONV_WIDTH = 4
RG_C = 8.0
DILATION_PAIRS = ((128, 1), (512, 4), (2048, 16))
GELU_C = math.sqrt(2.0 / math.pi)

LANES = 128
SUBLANES = 8
MXU_DIM = 256
VMEM_LIMIT_BYTES = 56 * 1024 * 1024

DIFF_TK = 256
DIFF_UNROLL = 8
BF16_SUBLANES = 16
ONES_ROWS = BF16_SUBLANES

LOG2E = math.log2(math.e)
Q_SCALE = HEAD_DIM ** -0.5 * LOG2E

_NT = (((1,), (1,)), ((), ()))


def _pick(n, prefs):
    for p in prefs:
        if n % p == 0:
            return p
    return n


def _cparams(sem):
    return pltpu.CompilerParams(dimension_semantics=sem, vmem_limit_bytes=VMEM_LIMIT_BYTES)


def _rms(x, g):
    return x * lax.rsqrt(jnp.mean(x * x, axis=-1, keepdims=True) + EPS) * g


def _norm_matmul_kernel(x_ref, g_ref, w_ref, o_ref, h_ref):
    @pl.when(pl.program_id(1) == 0)
    def _():
        h_ref[...] = _rms(x_ref[...], g_ref[...]).astype(BF16)

    o_ref[...] = jnp.dot(h_ref[...], w_ref[...], preferred_element_type=F32).astype(o_ref.dtype)


def norm_matmul(x, g, w, out_dtype):
    t, d = x.shape
    n = w.shape[1]
    tm = _pick(t, (512, 256, 128))
    tn = _pick(n, (1024, 512, 256, 128))
    return pl.pallas_call(
        _norm_matmul_kernel,
        grid=(t // tm, n // tn),
        in_specs=[pl.BlockSpec((tm, d), lambda i, j: (i, 0)),
                  pl.BlockSpec((1, d), lambda i, j: (0, 0)),
                  pl.BlockSpec((d, tn), lambda i, j: (0, j))],
        out_specs=pl.BlockSpec((tm, tn), lambda i, j: (i, j)),
        out_shape=jax.ShapeDtypeStruct((t, n), out_dtype),
        scratch_shapes=[pltpu.VMEM((tm, d), BF16)],
        compiler_params=_cparams(("parallel", "arbitrary")),
        name="norm_matmul",
    )(x, g, w)


def _qkv_kernel(x_ref, g_ref, w_ref, c_ref, sa_ref, sb_ref, o_ref, h_ref, *, n_q, n_rope):
    j = pl.program_id(1)

    @pl.when(j == 0)
    def _():
        h_ref[...] = _rms(x_ref[...], g_ref[...]).astype(BF16)

    acc = jnp.dot(h_ref[...], w_ref[...], preferred_element_type=F32)
    tn = acc.shape[1]

    @pl.when(j < n_rope)
    def _():
        scale = jnp.where(j < n_q, Q_SCALE, 1.0).astype(F32)
        c = c_ref[...] * scale
        sa = sa_ref[...] * scale
        sb = sb_ref[...] * scale
        for cb in range(tn // LANES):
            a = acc[:, cb * LANES:(cb + 1) * LANES]
            r = (a * c + pltpu.roll(a, LANES - ROT_HALF, 1) * sa
                 + pltpu.roll(a, ROT_HALF, 1) * sb)
            o_ref[:, cb * LANES:(cb + 1) * LANES] = r.astype(o_ref.dtype)

    @pl.when(j >= n_rope)
    def _():
        o_ref[...] = acc.astype(o_ref.dtype)


def qkv_proj(x, g, w, rope, seq, n_parts):
    t, d = x.shape
    n = w.shape[1]
    width = n // n_parts
    tm = _pick(seq, (512, 256, 128))
    tn = _pick(width, (1024, 512, 256, 128))
    n_q = width // tn
    c, sa, sb = rope
    nblk_s = seq // tm
    tab_spec = pl.BlockSpec((tm, LANES), lambda i, j: (i % nblk_s, 0))
    return pl.pallas_call(
        functools.partial(_qkv_kernel, n_q=n_q, n_rope=2 * n_q),
        grid=(t // tm, n // tn),
        in_specs=[pl.BlockSpec((tm, d), lambda i, j: (i, 0)),
                  pl.BlockSpec((1, d), lambda i, j: (0, 0)),
                  pl.BlockSpec((d, tn), lambda i, j: (0, j)),
                  tab_spec, tab_spec, tab_spec],
        out_specs=pl.BlockSpec((tm, tn), lambda i, j: (i, j)),
        out_shape=jax.ShapeDtypeStruct((t, n), BF16),
        scratch_shapes=[pltpu.VMEM((tm, d), BF16)],
        compiler_params=_cparams(("parallel", "arbitrary")),
        name="qkv_proj",
    )(x, g, w, c, sa, sb)


def rope_tables(seq):
    pos = jnp.arange(seq, dtype=F32)
    inv = ROPE_THETA ** (-jnp.arange(0, ROT_DIM, 2, dtype=F32) / ROT_DIM)
    ang = pos[:, None] * inv[None, :]
    cos, sin = jnp.cos(ang), jnp.sin(ang)
    zeros = jnp.zeros((seq, HEAD_DIM - ROT_DIM), F32)
    zh = jnp.zeros((seq, ROT_HALF), F32)
    c = jnp.concatenate([cos, cos, zeros + 1.0], axis=1)
    sa = jnp.concatenate([-sin, zh, zeros], axis=1)
    sb = jnp.concatenate([zh, sin, zeros], axis=1)
    rep = LANES // HEAD_DIM
    return tuple(jnp.tile(tab, (1, rep)) for tab in (c, sa, sb))


def _shift_rows(x, s, fill, reverse):
    n = x.shape[0]
    if s % SUBLANES == 0:
        pad = jnp.full((s, x.shape[1]), fill, x.dtype)
        return (jnp.concatenate([x[s:], pad], axis=0) if reverse
                else jnp.concatenate([pad, x[:n - s]], axis=0))
    row = lax.broadcasted_iota(jnp.int32, x.shape, 0)
    if reverse:
        return jnp.where(row < n - s, pltpu.roll(x, n - s, 0), fill)
    return jnp.where(row >= s, pltpu.roll(x, s, 0), fill)


def _rglru_kernel(*refs, reverse, final, n_blk):
    if final:
        (xp_ref, xc_ref, xn_ref, cw_ref, cb_ref, wg_ref, br_ref, bi_ref, lam_ref,
         gate_ref, prev_ref, o_ref, carry_ref) = refs
    else:
        (xp_ref, xc_ref, xn_ref, cw_ref, cb_ref, wg_ref, br_ref, bi_ref, lam_ref,
         o_ref, carry_ref) = refs
    i = pl.program_id(1)
    blk = n_blk - 1 - i if reverse else i
    tb, aw = xc_ref.shape

    @pl.when(i == 0)
    def _():
        carry_ref[...] = jnp.zeros_like(carry_ref)

    x0 = xc_ref[...]
    xp = jnp.where(blk > 0, xp_ref[...], 0.0)
    xn = jnp.where(blk < n_blk - 1, xn_ref[...], 0.0)
    row = lax.broadcasted_iota(jnp.int32, (tb, aw), 0)
    zpad = jnp.zeros((tb - SUBLANES, aw), F32)
    conv = cb_ref[...] + cw_ref[2:3, :] * x0
    for back in (1, 2):
        head = jnp.concatenate([pltpu.roll(xp, back, 0), zpad], axis=0)
        shifted = jnp.where(row < back, head, pltpu.roll(x0, back, 0))
        conv = conv + cw_ref[2 - back:3 - back, :] * shifted
    tail = jnp.concatenate([zpad, pltpu.roll(xn, SUBLANES - 1, 0)], axis=0)
    shifted = jnp.where(row == tb - 1, tail, pltpu.roll(x0, tb - 1, 0))
    conv = conv + cw_ref[3:4, :] * shifted

    n_grp, gw, _ = wg_ref.shape
    cbf = conv.astype(BF16)
    zr, zi = [], []
    for g in range(n_grp):
        z = jnp.dot(cbf[:, g * gw:(g + 1) * gw], wg_ref[g], preferred_element_type=F32)
        zr.append(z[:, :gw])
        zi.append(z[:, gw:])
    zr = jnp.concatenate(zr, axis=1) if n_grp > 1 else zr[0]
    zi = jnp.concatenate(zi, axis=1) if n_grp > 1 else zi[0]
    r = 1.0 / (1.0 + jnp.exp(-(zr + br_ref[...])))
    ig = 1.0 / (1.0 + jnp.exp(-(zi + bi_ref[...])))
    nl = -lam_ref[...]
    softplus = jnp.maximum(nl, 0.0) + jnp.log(1.0 + jnp.exp(-jnp.abs(nl)))
    a = jnp.exp(-RG_C * r * softplus)
    u = jnp.sqrt(jnp.maximum(1.0 - a * a, 0.0)) * (ig * conv)

    s = 1
    while s < tb:
        a_sh = _shift_rows(a, s, 1.0, reverse)
        u_sh = _shift_rows(u, s, 0.0, reverse)
        u = a * u_sh + u
        a = a * a_sh
        s *= 2
    edge = 0 if reverse else SUBLANES - 1
    h = a * carry_ref[edge:edge + 1, :] + u
    carry_ref[...] = h[:SUBLANES] if reverse else h[tb - SUBLANES:]

    if final:
        gt = gate_ref[...]
        gelu = 0.5 * gt * (1.0 + jnp.tanh(GELU_C * (gt + 0.044715 * (gt * gt * gt))))
        o_ref[...] = ((prev_ref[...] + h) * gelu).astype(o_ref.dtype)
    else:
        o_ref[...] = h


def rglru_direction(z_a, conv_w, conv_b, w_gate, b_r, b_i, lam, prev, reverse):
    bsz, seq, two_aw = z_a.shape
    aw = two_aw // 2
    tb = _pick(seq, (256, 128, 64, 32, 16, 8))
    n_blk = seq // tb
    rows8 = seq // SUBLANES
    per8 = tb // SUBLANES
    final = prev is not None

    def blk(i):
        return n_blk - 1 - i if reverse else i

    xr_specs = [
        pl.BlockSpec((None, SUBLANES, aw), lambda b, i: (b, jnp.maximum(blk(i) * per8 - 1, 0), 1)),
        pl.BlockSpec((None, tb, aw), lambda b, i: (b, blk(i), 1)),
        pl.BlockSpec((None, SUBLANES, aw),
                     lambda b, i: (b, jnp.minimum((blk(i) + 1) * per8, rows8 - 1), 1)),
    ]
    full = lambda a: pl.BlockSpec(a.shape, lambda b, i: (0,) * a.ndim)
    params = [conv_w, conv_b, w_gate, b_r, b_i, lam]
    in_specs = xr_specs + [full(p) for p in params]
    args = [z_a, z_a, z_a] + params
    if final:
        in_specs += [pl.BlockSpec((None, tb, aw), lambda b, i: (b, blk(i), 0)),
                     pl.BlockSpec((None, tb, aw), lambda b, i: (b, blk(i), 0))]
        args += [z_a, prev]
    return pl.pallas_call(
        functools.partial(_rglru_kernel, reverse=reverse, final=final, n_blk=n_blk),
        grid=(bsz, n_blk),
        in_specs=in_specs,
        out_specs=pl.BlockSpec((None, tb, aw), lambda b, i: (b, blk(i), 0)),
        out_shape=jax.ShapeDtypeStruct((bsz, seq, aw), BF16 if final else F32),
        scratch_shapes=[pltpu.VMEM((SUBLANES, aw), F32)],
        compiler_params=_cparams(("parallel", "arbitrary")),
        name="rglru_bwd" if reverse else "rglru_fwd",
    )(*args)


def gate_group_weights(w_r, w_i):
    n, bs, _ = w_r.shape
    aw = n * bs
    gw = _pick(aw, (MXU_DIM, LANES))
    per = gw // bs
    n_grp = aw // gw
    eye = jnp.eye(per, dtype=w_r.dtype)

    def dense(w):
        wg = w.reshape(n_grp, per, bs, bs)
        return jnp.einsum('gaef,ab->gaebf', wg, eye).reshape(n_grp, gw, gw)

    return jnp.concatenate([dense(w_r), dense(w_i)], axis=2).astype(BF16)


DIL_TILE = 128
DIL_UNROLL = 8


def _dilated_kernel(q_ref, kp_ref, kc_ref, kn_ref, vp_ref, vc_ref, vn_ref, o_ref,
                    qbuf, kbuf, vbuf, so_ref, sl_ref, *, seq, half, dilations):
    tb = q_ref.shape[0]
    halo = kp_ref.shape[0]
    t0 = pl.program_id(1) * tb
    qbuf[...] = q_ref[...].astype(F32)
    for buf, prev, cur, nxt in ((kbuf, kp_ref, kc_ref, kn_ref), (vbuf, vp_ref, vc_ref, vn_ref)):
        buf[0:halo] = prev[...].astype(F32)
        buf[halo:halo + tb] = cur[...].astype(F32)
        buf[halo + tb:halo + tb + halo] = nxt[...].astype(F32)

    nk = DIL_TILE + 2 * half
    rowq = lax.broadcasted_iota(jnp.int32, (DIL_TILE, nk), 0)
    colk = lax.broadcasted_iota(jnp.int32, (DIL_TILE, nk), 1)
    band_bias = jnp.where(jnp.abs(colk - half - rowq) <= half, 0.0, NEG).astype(F32)
    col1 = lax.broadcasted_iota(jnp.int32, (1, nk), 1)
    lane = lax.broadcasted_iota(jnp.int32, (DIL_TILE, LANES), 1)
    lo = lane < HEAD_DIM

    for bi, d in enumerate(dilations):
        span = DIL_TILE * d

        def rows(ref, start, size, d=d):
            if d == 1:
                return ref[pl.ds(pl.multiple_of(start, SUBLANES), size), :]
            return ref[pl.ds(start, size, stride=d), :]

        def tile(n, carry, bi=bi, d=d, span=span, rows=rows):
            g, r = n // d, n % d
            qs = g * span + r
            ks = halo + qs - half * d
            tok = (t0 - halo + ks) + col1 * d
            bias = band_bias + jnp.where((tok >= 0) & (tok < seq), 0.0, NEG)
            q = rows(qbuf, qs, DIL_TILE).astype(BF16)
            k = rows(kbuf, ks, nk).astype(BF16)
            v = rows(vbuf, ks, nk).astype(BF16)
            outs, lses = [], []
            for first in (True, False):
                qm = jnp.where(lo if first else ~lo, q, jnp.zeros_like(q))
                s = lax.dot_general(qm, k, _NT, preferred_element_type=F32) + bias
                m = jnp.max(s, axis=1, keepdims=True)
                p = jnp.exp2(s - m)
                den = jnp.sum(p, axis=1, keepdims=True)
                pv = jnp.dot(p.astype(BF16), v, preferred_element_type=F32)
                outs.append(pv / den)
                lses.append(m + jnp.log2(den))
            o = jnp.where(lo, outs[0], outs[1])
            lse = jnp.where(lo, lses[0], lses[1])
            if d == 1:
                dst = pl.ds(pl.multiple_of(qs, SUBLANES), DIL_TILE)
            else:
                dst = pl.ds(qs, DIL_TILE, stride=d)
            if bi > 0:
                lp = sl_ref[dst, :]
                mx = jnp.maximum(lp, lse)
                tot = mx + jnp.log2(jnp.exp2(lp - mx) + jnp.exp2(lse - mx))
                o = so_ref[dst, :] * jnp.exp2(lp - tot) + o * jnp.exp2(lse - tot)
                lse = tot
            so_ref[dst, :] = o
            if bi < len(dilations) - 1:
                sl_ref[dst, :] = lse
            return carry

        lax.fori_loop(0, tb // DIL_TILE, tile, 0, unroll=DIL_UNROLL)

    o_ref[...] = so_ref[...].astype(o_ref.dtype)


def dilated_mixture(qkv):
    bsz, seq, three_w = qkv.shape
    width = three_w // 3
    n_pairs = width // LANES
    dilations = tuple(d for _, d in DILATION_PAIRS)
    halves = {w // (2 * d) for w, d in DILATION_PAIRS}
    assert len(halves) == 1
    half = halves.pop()
    halo = half * max(dilations)
    tb = DIL_TILE * max(dilations)
    assert seq % tb == 0 and tb % halo == 0 and all(max(dilations) % d == 0 for d in dilations)
    per = tb // halo
    n_halo = seq // halo

    def centre(col):
        return pl.BlockSpec((None, tb, LANES), lambda b, i, p: (b, i, col * n_pairs + p))

    def before(col):
        return pl.BlockSpec((None, halo, LANES),
                            lambda b, i, p: (b, jnp.maximum(i * per - 1, 0), col * n_pairs + p))

    def after(col):
        return pl.BlockSpec((None, halo, LANES),
                            lambda b, i, p: (b, jnp.minimum((i + 1) * per, n_halo - 1),
                                             col * n_pairs + p))

    return pl.pallas_call(
        functools.partial(_dilated_kernel, seq=seq, half=half, dilations=dilations),
        grid=(bsz, seq // tb, n_pairs),
        in_specs=[centre(0), before(1), centre(1), after(1), before(2), centre(2), after(2)],
        out_specs=pl.BlockSpec((None, tb, LANES), lambda b, i, p: (b, i, p)),
        out_shape=jax.ShapeDtypeStruct((bsz, seq, width), BF16),
        scratch_shapes=[pltpu.VMEM((tb, LANES), F32),
                        pltpu.VMEM((tb + 2 * halo, LANES), F32),
                        pltpu.VMEM((tb + 2 * halo, LANES), F32),
                        pltpu.VMEM((tb, LANES), F32),
                        pltpu.VMEM((tb, LANES), F32)],
        compiler_params=_cparams(("parallel", "parallel", "parallel")),
        name="dilated_mixture",
    )(*([qkv] * 7))


def _norm_matmul_t_kernel(x_ref, g_ref, w_ref, o_ref, h_ref):
    @pl.when(pl.program_id(1) == 0)
    def _():
        h_ref[...] = _rms(x_ref[...], g_ref[...]).astype(BF16)

    acc_t = jnp.dot(h_ref[...], w_ref[...], preferred_element_type=F32).T
    n_chunk, _, tk = o_ref.shape
    for c in range(n_chunk):
        o_ref[c] = acc_t[:, c * tk:(c + 1) * tk].astype(o_ref.dtype)


def norm_matmul_t(x, g, w, bsz, seq, tk):
    t, d = x.shape
    n = w.shape[1]
    tm = _pick(seq, (512, 256, 128))
    tn = _pick(n, (512, 256, 128))
    assert tm % tk == 0
    per = tm // tk
    nblk_s = seq // tm
    return pl.pallas_call(
        _norm_matmul_t_kernel,
        grid=(t // tm, n // tn),
        in_specs=[pl.BlockSpec((tm, d), lambda i, j: (i, 0)),
                  pl.BlockSpec((1, d), lambda i, j: (0, 0)),
                  pl.BlockSpec((d, tn), lambda i, j: (0, j))],
        out_specs=pl.BlockSpec((None, per, tn, tk), lambda i, j: (i // nblk_s, i % nblk_s, j, 0)),
        out_shape=jax.ShapeDtypeStruct((bsz, seq // tk, n, tk), BF16),
        scratch_shapes=[pltpu.VMEM((tm, d), BF16)],
        compiler_params=_cparams(("parallel", "arbitrary")),
        name="norm_matmul_t",
    )(x, g, w)


def _diff_attn_kernel(q_ref, k_ref, vt_ref, lq1_ref, lk1_ref, lq2_ref, lk2_ref, sub_ref, o_ref,
                      sa_ref, sb_ref, acc_ref, *, lam_init):
    tq = q_ref.shape[0]
    n_kt, _, tk = vt_ref.shape
    q = q_ref[...]
    lane = lax.broadcasted_iota(jnp.int32, (tq, LANES), 1)
    lo = lane < HEAD_DIM
    q1 = jnp.where(lo, q, jnp.zeros_like(q))
    q2 = jnp.where(lo, jnp.zeros_like(q), q)

    def scores(kt, s_ref):
        k = k_ref[pl.ds(pl.multiple_of(kt * tk, tk), tk), :]
        s_ref[0] = lax.dot_general(k, q1, _NT, preferred_element_type=F32)
        s_ref[1] = lax.dot_general(k, q2, _NT, preferred_element_type=F32)

    ones = jnp.ones((ONES_ROWS, tk), BF16)

    def absorb(kt, s_ref, ms):
        vt = jnp.concatenate([vt_ref[kt], ones], axis=0)
        out = []
        for j in range(2):
            st = s_ref[j]
            m_new = jnp.maximum(ms[j], jnp.max(st, axis=0, keepdims=True))
            alpha = jnp.exp2(ms[j] - m_new)
            p = jnp.exp2((st - m_new).astype(BF16))
            acc_ref[j] = alpha * acc_ref[j] + jnp.dot(vt, p, preferred_element_type=F32)
            out.append(m_new)
        return tuple(out)

    def body(it, ms):
        kt = DIFF_UNROLL * it
        bufs = (sa_ref, sb_ref)
        for u in range(DIFF_UNROLL):
            scores(jnp.minimum(kt + u + 1, n_kt - 1), bufs[(u + 1) % 2])
            ms = absorb(kt + u, bufs[u % 2], ms)
        return ms

    m0 = jnp.full((1, tq), NEG, F32)
    acc_ref[...] = jnp.zeros_like(acc_ref)
    scores(0, sa_ref)
    lax.fori_loop(0, n_kt // DIFF_UNROLL, body, (m0, m0))
    a1, l1 = acc_ref[0, :LANES], acc_ref[0, LANES:LANES + 1]
    a2, l2 = acc_ref[1, :LANES], acc_ref[1, LANES:LANES + 1]

    lam = (jnp.exp(jnp.sum(lq1_ref[...] * lk1_ref[...], axis=1, keepdims=True))
           - jnp.exp(jnp.sum(lq2_ref[...] * lk2_ref[...], axis=1, keepdims=True)) + lam_init)
    o_t = a1 * (1.0 / l1) - lam * (a2 * (1.0 / l2))
    o_t = o_t * (lax.rsqrt(jnp.mean(o_t * o_t, axis=0, keepdims=True) + EPS) * (1.0 - lam_init))
    o_ref[...] = (o_t.T * sub_ref[...]).astype(o_ref.dtype)


def diff_attention(qk, vt, lq1, lk1, lq2, lk2, subln, lam_init):
    bsz, seq, two_d = qk.shape
    d = two_d // 2
    n_heads = d // LANES
    n_kt, tk = vt.shape[1], vt.shape[3]
    assert DIFF_UNROLL % 2 == 0 and n_kt % DIFF_UNROLL == 0
    tq = _pick(seq, (512, 256, 128))
    row = lambda a: a.reshape(1, -1).astype(F32)
    small = lambda n: pl.BlockSpec((1, n), lambda b, h, i: (0, 0))
    return pl.pallas_call(
        functools.partial(_diff_attn_kernel, lam_init=lam_init),
        grid=(bsz, n_heads, seq // tq),
        in_specs=[pl.BlockSpec((None, tq, LANES), lambda b, h, i: (b, i, h)),
                  pl.BlockSpec((None, seq, LANES), lambda b, h, i: (b, 0, n_heads + h)),
                  pl.BlockSpec((None, n_kt, LANES, tk), lambda b, h, i: (b, 0, h, 0)),
                  small(HEAD_DIM), small(HEAD_DIM), small(HEAD_DIM), small(HEAD_DIM),
                  small(LANES)],
        out_specs=pl.BlockSpec((None, tq, LANES), lambda b, h, i: (b, i, h)),
        out_shape=jax.ShapeDtypeStruct((bsz, seq, d), BF16),
        scratch_shapes=[pltpu.VMEM((2, tk, tq), F32), pltpu.VMEM((2, tk, tq), F32),
                        pltpu.VMEM((2, LANES + ONES_ROWS, tq), F32)],
        compiler_params=_cparams(("parallel", "parallel", "parallel")),
        name="diff_attention",
    )(qk, qk, vt, row(lq1), row(lk1), row(lq2), row(lk2), row(subln))


def _out_proj_kernel(*refs, n_in):
    ys = refs[:n_in]
    ws = refs[n_in:2 * n_in]
    x_ref, g_ref, o_ref = refs[2 * n_in:]
    m = jnp.dot(ys[0][...], ws[0][...], preferred_element_type=F32)
    for y_ref, w_ref in zip(ys[1:], ws[1:]):
        m = m + jnp.dot(y_ref[...], w_ref[...], preferred_element_type=F32)
    o_ref[...] = x_ref[...] + _rms(m, g_ref[...])


def out_proj_residual(ys, ws, x, g):
    t, d = x.shape
    tm = _pick(t, (512, 256, 128))
    in_specs = ([pl.BlockSpec((tm, y.shape[1]), lambda i: (i, 0)) for y in ys]
                + [pl.BlockSpec(w.shape, lambda i: (0, 0)) for w in ws]
                + [pl.BlockSpec((tm, d), lambda i: (i, 0)), pl.BlockSpec((1, d), lambda i: (0, 0))])
    return pl.pallas_call(
        functools.partial(_out_proj_kernel, n_in=len(ys)),
        grid=(t // tm,),
        in_specs=in_specs,
        out_specs=pl.BlockSpec((tm, d), lambda i: (i, 0)),
        out_shape=jax.ShapeDtypeStruct((t, d), F32),
        compiler_params=_cparams(("parallel",)),
        name="out_proj",
    )(*ys, *ws, x, g)


def _ffn_kernel(x_ref, gpre_ref, gpost_ref, wg_ref, wu_ref, wd_ref, o_ref, h_ref, acc_ref):
    f = pl.program_id(1)

    @pl.when(f == 0)
    def _():
        h_ref[...] = _rms(x_ref[...], gpre_ref[...]).astype(BF16)
        acc_ref[...] = jnp.zeros_like(acc_ref)

    h = h_ref[...]
    gate = jnp.dot(h, wg_ref[...], preferred_element_type=F32)
    up = jnp.dot(h, wu_ref[...], preferred_element_type=F32)
    act = (gate / (1.0 + jnp.exp(-gate))) * up
    acc_ref[...] += jnp.dot(act.astype(BF16), wd_ref[...], preferred_element_type=F32)

    @pl.when(f == pl.num_programs(1) - 1)
    def _():
        o_ref[...] = x_ref[...] + _rms(acc_ref[...], gpost_ref[...])


def ffn_residual(x, g_pre, g_post, w_gate, w_up, w_down):
    t, d = x.shape
    f = w_gate.shape[1]
    tm = _pick(t, (512, 256, 128))
    tf = _pick(f, (512, 256, 128))
    return pl.pallas_call(
        _ffn_kernel,
        grid=(t // tm, f // tf),
        in_specs=[pl.BlockSpec((tm, d), lambda i, j: (i, 0)),
                  pl.BlockSpec((1, d), lambda i, j: (0, 0)),
                  pl.BlockSpec((1, d), lambda i, j: (0, 0)),
                  pl.BlockSpec((d, tf), lambda i, j: (0, j)),
                  pl.BlockSpec((d, tf), lambda i, j: (0, j)),
                  pl.BlockSpec((tf, d), lambda i, j: (j, 0))],
        out_specs=pl.BlockSpec((tm, d), lambda i, j: (i, 0)),
        out_shape=jax.ShapeDtypeStruct((t, d), F32),
        scratch_shapes=[pltpu.VMEM((tm, d), BF16), pltpu.VMEM((tm, d), F32)],
        compiler_params=_cparams(("parallel", "arbitrary")),
        name="ffn",
    )(x, g_pre, g_post, w_gate, w_up, w_down)


def kernel(x, norm_mix_pre, norm_mix_post, norm_ffn_pre, norm_ffn_post, ev_w_in, ev_conv_w, ev_conv_b, ev_w_r, ev_b_r, ev_w_i, ev_b_i, ev_lam, ev_w_out, od_w_in, od_lam_q1, od_lam_k1, od_lam_q2, od_lam_k2, od_subln, od_w_out, ffn_w_gate, ffn_w_up, ffn_w_down):
    bsz, seq, d = x.shape
    depth = norm_mix_pre.shape[0]
    t = bsz * seq
    aw = ev_conv_w.shape[-1]
    rope = rope_tables(seq)
    gain = lambda g: g.reshape(1, d).astype(F32)
    xf = x.reshape(t, d)
    for layer in range(depth):
        j = layer // 2
        g_pre = gain(norm_mix_pre[layer])
        if layer % 2 == 0:
            w_in = ev_w_in[j].astype(BF16)
            z_a = norm_matmul(xf, g_pre, w_in[:, :2 * aw], F32).reshape(bsz, seq, 2 * aw)
            qkv = qkv_proj(xf, g_pre, w_in[:, 2 * aw:], rope, seq, 3)
            rows =lambda p: p.reshape(1, aw).astype(F32)
            h_fwd = None
            for direction in (0, 1):
                h_fwd = rglru_direction(
                    z_a, ev_conv_w[j].astype(F32), rows(ev_conv_b[j]),
                    gate_group_weights(ev_w_r[j, direction], ev_w_i[j, direction]),
                    rows(ev_b_r[j, direction]), rows(ev_b_i[j, direction]),
                    rows(ev_lam[j, direction]), h_fwd, reverse=direction == 1)
            y_a = h_fwd.reshape(t, aw)
            y_b = dilated_mixture(qkv.reshape(bsz, seq, -1)).reshape(t, -1)
            w_out = ev_w_out[j].astype(BF16)
            ys, ws = [y_a, y_b], [w_out[:aw], w_out[aw:]]
        else:
            lam_init = 0.8 - 0.6 * math.exp(-0.3 * layer)
            w_in = od_w_in[j].astype(BF16)
            qk = qkv_proj(xf, g_pre, w_in[:, :2 * d], rope, seq, 2)
            vt = norm_matmul_t(xf, g_pre, w_in[:, 2 * d:], bsz, seq, _pick(seq, (DIFF_TK, LANES)))
            y = diff_attention(qk.reshape(bsz, seq, -1), vt, od_lam_q1[j], od_lam_k1[j],
                               od_lam_q2[j], od_lam_k2[j], od_subln[j], lam_init)
            ys, ws = [y.reshape(t, -1)], [od_w_out[j].astype(BF16)]
        xf = out_proj_residual(ys, ws, xf, gain(norm_mix_post[layer]))
        xf = ffn_residual(xf, gain(norm_ffn_pre[layer]), gain(norm_ffn_post[layer]),
                          ffn_w_gate[layer].astype(BF16), ffn_w_up[layer].astype(BF16),
                          ffn_w_down[layer].astype(BF16))
    return xf.reshape(bsz, seq, d)
```

```python
import functools
import math

import jax
import jax.numpy as jnp
from jax import lax
from jax.experimental import pallas as pl
from jax.experimental.pallas import tpu as pltpu

F32 = jnp.float32
BF16 = jnp.bfloat16

EPS = 1e-6
NEG = -1e30
HEAD_DIM = 64
ROT_DIM = HEAD_DIM // 4
ROT_HALF = ROT_DIM // 2
ROPE_THETA = 500000.0
CONV_WIDTH = 4
RG_C = 8.0
DILATION_PAIRS = ((128, 1), (512, 4), (2048, 16))
GELU_C = math.sqrt(2.0 / math.pi)

LANES = 128
SUBLANES = 8
MXU_DIM = 256
VMEM_LIMIT_BYTES = 56 * 1024 * 1024

DIFF_TK = 256
DIFF_UNROLL = 8
BF16_SUBLANES = 16
ONES_ROWS = BF16_SUBLANES

LOG2E = math.log2(math.e)
Q_SCALE = HEAD_DIM ** -0.5 * LOG2E

_NT = (((1,), (1,)), ((), ()))


def _pick(n, prefs):
    for p in prefs:
        if n % p == 0:
            return p
    return n


def _cparams(sem):
    return pltpu.CompilerParams(dimension_semantics=sem, vmem_limit_bytes=VMEM_LIMIT_BYTES)


def _rms(x, g):
    return x * lax.rsqrt(jnp.mean(x * x, axis=-1, keepdims=True) + EPS) * g


def _norm_matmul_kernel(x_ref, g_ref, w_ref, o_ref):
    h = _rms(x_ref[...], g_ref[...]).astype(BF16)
    o_ref[...] = jnp.dot(h, w_ref[...], preferred_element_type=F32).astype(o_ref.dtype)


def _col_block(col0, tn):
    assert col0 % tn == 0
    return col0 // tn


def norm_matmul(x, g, w, col0, n, out_dtype):
    t, d = x.shape
    tm = _pick(t, (512, 256, 128))
    tn = _pick(n, (2048, 1024, 512, 256, 128))
    j0 = _col_block(col0, tn)
    return pl.pallas_call(
        _norm_matmul_kernel,
        grid=(t // tm, n // tn),
        in_specs=[pl.BlockSpec((tm, d), lambda i, j: (i, 0)),
                  pl.BlockSpec((1, d), lambda i, j: (0, 0)),
                  pl.BlockSpec((d, tn), lambda i, j: (0, j0 + j))],
        out_specs=pl.BlockSpec((tm, tn), lambda i, j: (i, j)),
        out_shape=jax.ShapeDtypeStruct((t, n), out_dtype),
        compiler_params=_cparams(("parallel", "parallel")),
        name="norm_matmul",
    )(x, g, w)


def _qkv_kernel(x_ref, g_ref, w_ref, c_ref, sa_ref, sb_ref, o_ref):
    h = _rms(x_ref[...], g_ref[...]).astype(BF16)
    acc = jnp.dot(h, w_ref[...], preferred_element_type=F32)
    c, sa, sb = c_ref[...], sa_ref[...], sb_ref[...]
    for cb in range(acc.shape[1] // LANES):
        a = acc[:, cb * LANES:(cb + 1) * LANES]
        r = a * c + pltpu.roll(a, LANES - ROT_HALF, 1) * sa + pltpu.roll(a, ROT_HALF, 1) * sb
        o_ref[:, cb * LANES:(cb + 1) * LANES] = r.astype(o_ref.dtype)


def qkv_proj(x, g, w, col0, n, rope, seq, n_parts):
    t, d = x.shape
    width = n // n_parts
    tm = _pick(seq, (512, 256, 128))
    tn = _pick(width, (2048, 1024, 512, 256, 128))
    per_part = width // tn
    j0 = _col_block(col0, tn)
    nblk_s = seq // tm
    tab_spec = pl.BlockSpec((None, tm, LANES), lambda i, j: (j // per_part, i % nblk_s, 0))
    return pl.pallas_call(
        _qkv_kernel,
        grid=(t // tm, n // tn),
        in_specs=[pl.BlockSpec((tm, d), lambda i, j: (i, 0)),
                  pl.BlockSpec((1, d), lambda i, j: (0, 0)),
                  pl.BlockSpec((d, tn), lambda i, j: (0, j0 + j)),
                  tab_spec, tab_spec, tab_spec],
        out_specs=pl.BlockSpec((tm, tn), lambda i, j: (i, j)),
        out_shape=jax.ShapeDtypeStruct((t, n), BF16),
        compiler_params=_cparams(("parallel", "parallel")),
        name="qkv_proj",
    )(x, g, w, *rope)


def rope_tables(seq):
    pos = jnp.arange(seq, dtype=F32)
    inv = ROPE_THETA ** (-jnp.arange(0, ROT_DIM, 2, dtype=F32) / ROT_DIM)
    ang = pos[:, None] * inv[None, :]
    cos, sin = jnp.cos(ang), jnp.sin(ang)
    zeros = jnp.zeros((seq, HEAD_DIM - ROT_DIM), F32)
    zh = jnp.zeros((seq, ROT_HALF), F32)
    c = jnp.concatenate([cos, cos, zeros + 1.0], axis=1)
    sa = jnp.concatenate([-sin, zh, zeros], axis=1)
    sb = jnp.concatenate([zh, sin, zeros], axis=1)
    rep = LANES // HEAD_DIM
    ident = (jnp.ones_like(c), jnp.zeros_like(c), jnp.zeros_like(c))
    return tuple(jnp.tile(jnp.stack([tab * Q_SCALE, tab, idt]), (1, 1, rep))
                 for tab, idt in zip((c, sa, sb), ident))


def _shift_rows(x, s, fill, reverse):
    n = x.shape[0]
    if s % SUBLANES == 0:
        pad = jnp.full((s, x.shape[1]), fill, x.dtype)
        return (jnp.concatenate([x[s:], pad], axis=0) if reverse
                else jnp.concatenate([pad, x[:n - s]], axis=0))
    row = lax.broadcasted_iota(jnp.int32, x.shape, 0)
    if reverse:
        return jnp.where(row < n - s, pltpu.roll(x, n - s, 0), fill)
    return jnp.where(row >= s, pltpu.roll(x, s, 0), fill)


def _rglru_kernel(*refs, reverse, final, n_blk):
    if final:
        (xp_ref, xc_ref, xn_ref, cw_ref, cb_ref, wg_ref, br_ref, bi_ref, lam_ref,
         gate_ref, prev_ref, o_ref, carry_ref) = refs
    else:
        (xp_ref, xc_ref, xn_ref, cw_ref, cb_ref, wg_ref, br_ref, bi_ref, lam_ref,
         o_ref, carry_ref) = refs
    i = pl.program_id(1)
    blk = n_blk - 1 - i if reverse else i
    tb, aw = xc_ref.shape

    @pl.when(i == 0)
    def _():
        carry_ref[...] = jnp.zeros_like(carry_ref)

    x0 = xc_ref[...]
    xp = jnp.where(blk > 0, xp_ref[...], 0.0)
    xn = jnp.where(blk < n_blk - 1, xn_ref[...], 0.0)
    row = lax.broadcasted_iota(jnp.int32, (tb, aw), 0)
    zpad = jnp.zeros((tb - SUBLANES, aw), F32)
    conv = cb_ref[...] + cw_ref[2:3, :] * x0
    for back in (1, 2):
        head = jnp.concatenate([pltpu.roll(xp, back, 0), zpad], axis=0)
        shifted = jnp.where(row < back, head, pltpu.roll(x0, back, 0))
        conv = conv + cw_ref[2 - back:3 - back, :] * shifted
    tail = jnp.concatenate([zpad, pltpu.roll(xn, SUBLANES - 1, 0)], axis=0)
    shifted = jnp.where(row == tb - 1, tail, pltpu.roll(x0, tb - 1, 0))
    conv = conv + cw_ref[3:4, :] * shifted

    n_grp, gw, _ = wg_ref.shape
    cbf = conv.astype(BF16)
    zr, zi = [], []
    for g in range(n_grp):
        z = jnp.dot(cbf[:, g * gw:(g + 1) * gw], wg_ref[g], preferred_element_type=F32)
        zr.append(z[:, :gw])
        zi.append(z[:, gw:])
    zr = jnp.concatenate(zr, axis=1) if n_grp > 1 else zr[0]
    zi = jnp.concatenate(zi, axis=1) if n_grp > 1 else zi[0]
    r = 1.0 / (1.0 + jnp.exp(-(zr + br_ref[...])))
    ig = 1.0 / (1.0 + jnp.exp(-(zi + bi_ref[...])))
    nl = -lam_ref[...]
    softplus = jnp.maximum(nl, 0.0) + jnp.log(1.0 + jnp.exp(-jnp.abs(nl)))
    a = jnp.exp(-RG_C * r * softplus)
    u = jnp.sqrt(jnp.maximum(1.0 - a * a, 0.0)) * (ig * conv)

    s = 1
    while s < tb:
        a_sh = _shift_rows(a, s, 1.0, reverse)
        u_sh = _shift_rows(u, s, 0.0, reverse)
        u = a * u_sh + u
        a = a * a_sh
        s *= 2
    edge = 0 if reverse else SUBLANES - 1
    h = a * carry_ref[edge:edge + 1, :] + u
    carry_ref[...] = h[:SUBLANES] if reverse else h[tb - SUBLANES:]

    if final:
        gt = gate_ref[...]
        gelu = 0.5 * gt * (1.0 + jnp.tanh(GELU_C * (gt + 0.044715 * (gt * gt * gt))))
        o_ref[...] = ((prev_ref[...] + h) * gelu).astype(o_ref.dtype)
    else:
        o_ref[...] = h


def rglru_direction(z_a, conv_w, conv_b, w_gate, b_r, b_i, lam, prev, reverse):
    bsz, seq, two_aw = z_a.shape
    aw = two_aw // 2
    tb = _pick(seq, (256, 128, 64, 32, 16, 8))
    n_blk = seq // tb
    rows8 = seq // SUBLANES
    per8 = tb // SUBLANES
    final = prev is not None

    def blk(i):
        return n_blk - 1 - i if reverse else i

    xr_specs = [
        pl.BlockSpec((None, SUBLANES, aw), lambda b, i: (b, jnp.maximum(blk(i) * per8 - 1, 0), 1)),
        pl.BlockSpec((None, tb, aw), lambda b, i: (b, blk(i), 1)),
        pl.BlockSpec((None, SUBLANES, aw),
                     lambda b, i: (b, jnp.minimum((blk(i) + 1) * per8, rows8 - 1), 1)),
    ]
    full = lambda a: pl.BlockSpec(a.shape, lambda b, i: (0,) * a.ndim)
    params = [conv_w, conv_b, w_gate, b_r, b_i, lam]
    in_specs = xr_specs + [full(p) for p in params]
    args = [z_a, z_a, z_a] + params
    if final:
        in_specs += [pl.BlockSpec((None, tb, aw), lambda b, i: (b, blk(i), 0)),
                     pl.BlockSpec((None, tb, aw), lambda b, i: (b, blk(i), 0))]
        args += [z_a, prev]
    return pl.pallas_call(
        functools.partial(_rglru_kernel, reverse=reverse, final=final, n_blk=n_blk),
        grid=(bsz, n_blk),
        in_specs=in_specs,
        out_specs=pl.BlockSpec((None, tb, aw), lambda b, i: (b, blk(i), 0)),
        out_shape=jax.ShapeDtypeStruct((bsz, seq, aw), BF16 if final else F32),
        scratch_shapes=[pltpu.VMEM((SUBLANES, aw), F32)],
        compiler_params=_cparams(("parallel", "arbitrary")),
        name="rglru_bwd" if reverse else "rglru_fwd",
    )(*args)


def gate_group_weights(w_r, w_i):
    n, bs, _ = w_r.shape
    aw = n * bs
    gw = _pick(aw, (MXU_DIM, LANES))
    per = gw // bs
    n_grp = aw // gw
    eye = jnp.eye(per, dtype=w_r.dtype)

    def dense(w):
        wg = w.reshape(n_grp, per, bs, bs)
        return jnp.einsum('gaef,ab->gaebf', wg, eye).reshape(n_grp, gw, gw)

    return jnp.concatenate([dense(w_r), dense(w_i)], axis=2).astype(BF16)


DIL_TILE = 128
DIL_UNROLL = 8


def _dilated_kernel(q_ref, kp_ref, kc_ref, kn_ref, vp_ref, vc_ref, vn_ref, o_ref,
                    qbuf, kbuf, vbuf, so_ref, sl_ref, *, seq, half, dilations):
    tb = q_ref.shape[0]
    halo = kp_ref.shape[0]
    t0 = pl.program_id(1) * tb
    qbuf[...] = q_ref[...].astype(F32)
    for buf, prev, cur, nxt in ((kbuf, kp_ref, kc_ref, kn_ref), (vbuf, vp_ref, vc_ref, vn_ref)):
        buf[0:halo] = prev[...].astype(F32)
        buf[halo:halo + tb] = cur[...].astype(F32)
        buf[halo + tb:halo + tb + halo] = nxt[...].astype(F32)

    nk = DIL_TILE + 2 * half
    rowq = lax.broadcasted_iota(jnp.int32, (DIL_TILE, nk), 0)
    colk = lax.broadcasted_iota(jnp.int32, (DIL_TILE, nk), 1)
    band_bias = jnp.where(jnp.abs(colk - half - rowq) <= half, 0.0, NEG).astype(F32)
    col1 = lax.broadcasted_iota(jnp.int32, (1, nk), 1)
    lane = lax.broadcasted_iota(jnp.int32, (DIL_TILE, LANES), 1)
    lo = lane < HEAD_DIM

    for bi, d in enumerate(dilations):
        span = DIL_TILE * d

        def rows(ref, start, size, d=d):
            if d == 1:
                return ref[pl.ds(pl.multiple_of(start, SUBLANES), size), :]
            return ref[pl.ds(start, size, stride=d), :]

        def tile(n, carry, bi=bi, d=d, span=span, rows=rows):
            g, r = n // d, n % d
            qs = g * span + r
            ks = halo + qs - half * d
            tok = (t0 - halo + ks) + col1 * d
            bias = band_bias + jnp.where((tok >= 0) & (tok < seq), 0.0, NEG)
            q = rows(qbuf, qs, DIL_TILE).astype(BF16)
            k = rows(kbuf, ks, nk).astype(BF16)
            v = rows(vbuf, ks, nk).astype(BF16)
            outs, lses = [], []
            for first in (True, False):
                qm = jnp.where(lo if first else ~lo, q, jnp.zeros_like(q))
                s = lax.dot_general(qm, k, _NT, preferred_element_type=F32) + bias
                m = jnp.max(s, axis=1, keepdims=True)
                p = jnp.exp2(s - m)
                den = jnp.sum(p, axis=1, keepdims=True)
                pv = jnp.dot(p.astype(BF16), v, preferred_element_type=F32)
                outs.append(pv / den)
                lses.append(m + jnp.log2(den))
            o = jnp.where(lo, outs[0], outs[1])
            lse = jnp.where(lo, lses[0], lses[1])
            if d == 1:
                dst = pl.ds(pl.multiple_of(qs, SUBLANES), DIL_TILE)
            else:
                dst = pl.ds(qs, DIL_TILE, stride=d)
            if bi > 0:
                lp = sl_ref[dst, :]
                mx = jnp.maximum(lp, lse)
                tot = mx + jnp.log2(jnp.exp2(lp - mx) + jnp.exp2(lse - mx))
                o = so_ref[dst, :] * jnp.exp2(lp - tot) + o * jnp.exp2(lse - tot)
                lse = tot
            so_ref[dst, :] = o
            if bi < len(dilations) - 1:
                sl_ref[dst, :] = lse
            return carry

        lax.fori_loop(0, tb // DIL_TILE, tile, 0, unroll=DIL_UNROLL)

    o_ref[...] = so_ref[...].astype(o_ref.dtype)


def dilated_mixture(qkv):
    bsz, seq, three_w = qkv.shape
    width = three_w // 3
    n_pairs = width // LANES
    dilations = tuple(d for _, d in DILATION_PAIRS)
    halves = {w // (2 * d) for w, d in DILATION_PAIRS}
    assert len(halves) == 1
    half = halves.pop()
    halo = half * max(dilations)
    tb = DIL_TILE * max(dilations)
    assert seq % tb == 0 and tb % halo == 0 and all(max(dilations) % d == 0 for d in dilations)
    per = tb // halo
    n_halo = seq // halo

    def centre(col):
        return pl.BlockSpec((None, tb, LANES), lambda b, i, p: (b, i, col * n_pairs + p))

    def before(col):
        return pl.BlockSpec((None, halo, LANES),
                            lambda b, i, p: (b, jnp.maximum(i * per - 1, 0), col * n_pairs + p))

    def after(col):
        return pl.BlockSpec((None, halo, LANES),
                            lambda b, i, p: (b, jnp.minimum((i + 1) * per, n_halo - 1),
                                             col * n_pairs + p))

    return pl.pallas_call(
        functools.partial(_dilated_kernel, seq=seq, half=half, dilations=dilations),
        grid=(bsz, seq // tb, n_pairs),
        in_specs=[centre(0), before(1), centre(1), after(1), before(2), centre(2), after(2)],
        out_specs=pl.BlockSpec((None, tb, LANES), lambda b, i, p: (b, i, p)),
        out_shape=jax.ShapeDtypeStruct((bsz, seq, width), BF16),
        scratch_shapes=[pltpu.VMEM((tb, LANES), F32),
                        pltpu.VMEM((tb + 2 * halo, LANES), F32),
                        pltpu.VMEM((tb + 2 * halo, LANES), F32),
                        pltpu.VMEM((tb, LANES), F32),
                        pltpu.VMEM((tb, LANES), F32)],
        compiler_params=_cparams(("parallel", "parallel", "parallel")),
        name="dilated_mixture",
    )(*([qkv] * 7))


def _norm_matmul_t_kernel(x_ref, g_ref, w_ref, o_ref):
    h = _rms(x_ref[...], g_ref[...]).astype(BF16)
    acc_t = jnp.dot(h, w_ref[...], preferred_element_type=F32).T
    n_chunk, _, tk = o_ref.shape
    for c in range(n_chunk):
        o_ref[c] = acc_t[:, c * tk:(c + 1) * tk].astype(o_ref.dtype)


def norm_matmul_t(x, g, w, col0, n, bsz, seq, tk):
    t, d = x.shape
    tm = _pick(seq, (512, 256, 128))
    tn = _pick(n, (1024, 512, 256, 128))
    j0 = _col_block(col0, tn)
    assert tm % tk == 0
    per = tm // tk
    nblk_s = seq // tm
    return pl.pallas_call(
        _norm_matmul_t_kernel,
        grid=(t // tm, n // tn),
        in_specs=[pl.BlockSpec((tm, d), lambda i, j: (i, 0)),
                  pl.BlockSpec((1, d), lambda i, j: (0, 0)),
                  pl.BlockSpec((d, tn), lambda i, j: (0, j0 + j))],
        out_specs=pl.BlockSpec((None, per, tn, tk), lambda i, j: (i // nblk_s, i % nblk_s, j, 0)),
        out_shape=jax.ShapeDtypeStruct((bsz, seq // tk, n, tk), BF16),
        compiler_params=_cparams(("parallel", "parallel")),
        name="norm_matmul_t",
    )(x, g, w)


def _diff_attn_kernel(q_ref, k_ref, vt_ref, lq1_ref, lk1_ref, lq2_ref, lk2_ref, sub_ref, o_ref,
                      sa_ref, sb_ref, acc_ref, *, lam_init):
    tq = q_ref.shape[0]
    n_kt, _, tk = vt_ref.shape
    q = q_ref[...]
    lane = lax.broadcasted_iota(jnp.int32, (tq, LANES), 1)
    lo = lane < HEAD_DIM
    q1 = jnp.where(lo, q, jnp.zeros_like(q))
    q2 = jnp.where(lo, jnp.zeros_like(q), q)

    def scores(kt, s_ref):
        k = k_ref[pl.ds(pl.multiple_of(kt * tk, tk), tk), :]
        s_ref[0] = lax.dot_general(k, q1, _NT, preferred_element_type=F32)
        s_ref[1] = lax.dot_general(k, q2, _NT, preferred_element_type=F32)

    ones = jnp.ones((ONES_ROWS, tk), BF16)

    def absorb(kt, s_ref, ms):
        vt = jnp.concatenate([vt_ref[kt], ones], axis=0)
        out = []
        for j in range(2):
            st = s_ref[j]
            m_new = jnp.maximum(ms[j], jnp.max(st, axis=0, keepdims=True))
            alpha = jnp.exp2(ms[j] - m_new)
            p = jnp.exp2((st - m_new).astype(BF16))
            acc_ref[j] = alpha * acc_ref[j] + jnp.dot(vt, p, preferred_element_type=F32)
            out.append(m_new)
        return tuple(out)

    def body(it, ms):
        kt = DIFF_UNROLL * it
        bufs = (sa_ref, sb_ref)
        for u in range(DIFF_UNROLL):
            scores(jnp.minimum(kt + u + 1, n_kt - 1), bufs[(u + 1) % 2])
            ms = absorb(kt + u, bufs[u % 2], ms)
        return ms

    m0 = jnp.full((1, tq), NEG, F32)
    acc_ref[...] = jnp.zeros_like(acc_ref)
    scores(0, sa_ref)
    lax.fori_loop(0, n_kt // DIFF_UNROLL, body, (m0, m0))
    a1, l1 = acc_ref[0, :LANES], acc_ref[0, LANES:LANES + 1]
    a2, l2 = acc_ref[1, :LANES], acc_ref[1, LANES:LANES + 1]

    lam = (jnp.exp(jnp.sum(lq1_ref[...] * lk1_ref[...], axis=1, keepdims=True))
           - jnp.exp(jnp.sum(lq2_ref[...] * lk2_ref[...], axis=1, keepdims=True)) + lam_init)
    o_t = a1 * (1.0 / l1) - lam * (a2 * (1.0 / l2))
    o_t = o_t * (lax.rsqrt(jnp.mean(o_t * o_t, axis=0, keepdims=True) + EPS) * (1.0 - lam_init))
    o_ref[...] = (o_t.T * sub_ref[...]).astype(o_ref.dtype)


def diff_attention(qk, vt, lq1, lk1, lq2, lk2, subln, lam_init):
    bsz, seq, two_d = qk.shape
    d = two_d // 2
    n_heads = d // LANES
    n_kt, tk = vt.shape[1], vt.shape[3]
    assert DIFF_UNROLL % 2 == 0 and n_kt % DIFF_UNROLL == 0
    tq = _pick(seq, (512, 256, 128))
    row = lambda a: a.reshape(1, -1).astype(F32)
    small = lambda n: pl.BlockSpec((1, n), lambda b, h, i: (0, 0))
    return pl.pallas_call(
        functools.partial(_diff_attn_kernel, lam_init=lam_init),
        grid=(bsz, n_heads, seq // tq),
        in_specs=[pl.BlockSpec((None, tq, LANES), lambda b, h, i: (b, i, h)),
                  pl.BlockSpec((None, seq, LANES), lambda b, h, i: (b, 0, n_heads + h)),
                  pl.BlockSpec((None, n_kt, LANES, tk), lambda b, h, i: (b, 0, h, 0)),
                  small(HEAD_DIM), small(HEAD_DIM), small(HEAD_DIM), small(HEAD_DIM),
                  small(LANES)],
        out_specs=pl.BlockSpec((None, tq, LANES), lambda b, h, i: (b, i, h)),
        out_shape=jax.ShapeDtypeStruct((bsz, seq, d), BF16),
        scratch_shapes=[pltpu.VMEM((2, tk, tq), F32), pltpu.VMEM((2, tk, tq), F32),
                        pltpu.VMEM((2, LANES + ONES_ROWS, tq), F32)],
        compiler_params=_cparams(("parallel", "parallel", "parallel")),
        name="diff_attention",
    )(qk, qk, vt, row(lq1), row(lk1), row(lq2), row(lk2), row(subln))


def _out_proj_kernel(*refs, n_in):
    ys = refs[:n_in]
    ws = refs[n_in:2 * n_in]
    x_ref, g_ref, o_ref = refs[2 * n_in:]
    m = jnp.dot(ys[0][...], ws[0][...], preferred_element_type=F32)
    for y_ref, w_ref in zip(ys[1:], ws[1:]):
        m = m + jnp.dot(y_ref[...], w_ref[...], preferred_element_type=F32)
    o_ref[...] = x_ref[...] + _rms(m, g_ref[...])


def out_proj_residual(ys, w, x, g):
    t, d = x.shape
    tm = _pick(t, (512, 256, 128))
    kdim = ys[0].shape[1]
    assert all(y.shape[1] == kdim for y in ys) and w.shape[0] == kdim * len(ys)
    in_specs = ([pl.BlockSpec((tm, kdim), lambda i: (i, 0)) for _ in ys]
                + [pl.BlockSpec((kdim, d), lambda i, r=r: (r, 0)) for r in range(len(ys))]
                + [pl.BlockSpec((tm, d), lambda i: (i, 0)), pl.BlockSpec((1, d), lambda i: (0, 0))])
    return pl.pallas_call(
        functools.partial(_out_proj_kernel, n_in=len(ys)),
        grid=(t // tm,),
        in_specs=in_specs,
        out_specs=pl.BlockSpec((tm, d), lambda i: (i, 0)),
        out_shape=jax.ShapeDtypeStruct((t, d), F32),
        compiler_params=_cparams(("parallel",)),
        name="out_proj",
    )(*ys, *([w] * len(ys)), x, g)


def _ffn_kernel(x_ref, gpre_ref, gpost_ref, wg_ref, wu_ref, wd_ref, o_ref, h_ref, acc_ref):
    f = pl.program_id(1)

    @pl.when(f == 0)
    def _():
        h_ref[...] = _rms(x_ref[...], gpre_ref[...]).astype(BF16)
        acc_ref[...] = jnp.zeros_like(acc_ref)

    h = h_ref[...]
    gate = jnp.dot(h, wg_ref[...], preferred_element_type=F32)
    up = jnp.dot(h, wu_ref[...], preferred_element_type=F32)
    act = (gate / (1.0 + jnp.exp(-gate))) * up
    acc_ref[...] += jnp.dot(act.astype(BF16), wd_ref[...], preferred_element_type=F32)

    @pl.when(f == pl.num_programs(1) - 1)
    def _():
        o_ref[...] = x_ref[...] + _rms(acc_ref[...], gpost_ref[...])


def ffn_residual(x, g_pre, g_post, w_gate, w_up, w_down):
    t, d = x.shape
    f = w_gate.shape[1]
    tm = _pick(t, (512, 256, 128))
    tf = _pick(f, (512, 256, 128))
    return pl.pallas_call(
        _ffn_kernel,
        grid=(t // tm, f // tf),
        in_specs=[pl.BlockSpec((tm, d), lambda i, j: (i, 0)),
                  pl.BlockSpec((1, d), lambda i, j: (0, 0)),
                  pl.BlockSpec((1, d), lambda i, j: (0, 0)),
                  pl.BlockSpec((d, tf), lambda i, j: (0, j)),
                  pl.BlockSpec((d, tf), lambda i, j: (0, j)),
                  pl.BlockSpec((tf, d), lambda i, j: (j, 0))],
        out_specs=pl.BlockSpec((tm, d), lambda i, j: (i, 0)),
        out_shape=jax.ShapeDtypeStruct((t, d), F32),
        scratch_shapes=[pltpu.VMEM((tm, d), BF16), pltpu.VMEM((tm, d), F32)],
        compiler_params=_cparams(("parallel", "arbitrary")),
        name="ffn",
    )(x, g_pre, g_post, w_gate, w_up, w_down)


def kernel(x, norm_mix_pre, norm_mix_post, norm_ffn_pre, norm_ffn_post, ev_w_in, ev_conv_w, ev_conv_b, ev_w_r, ev_b_r, ev_w_i, ev_b_i, ev_lam, ev_w_out, od_w_in, od_lam_q1, od_lam_k1, od_lam_q2, od_lam_k2, od_subln, od_w_out, ffn_w_gate, ffn_w_up, ffn_w_down):
    bsz, seq, d = x.shape
    depth = norm_mix_pre.shape[0]
    t = bsz * seq
    aw = ev_conv_w.shape[-1]
    rope = rope_tables(seq)
    gain = lambda g: g.reshape(1, d).astype(F32)
    xf = x.reshape(t, d)
    for layer in range(depth):
        j = layer // 2
        g_pre = gain(norm_mix_pre[layer])
        if layer % 2 == 0:
            w_in = ev_w_in[j].astype(BF16)
            z_a = norm_matmul(xf, g_pre, w_in, 0, 2 * aw, F32).reshape(bsz, seq, 2 * aw)
            qkv = qkv_proj(xf, g_pre, w_in, 2 * aw, w_in.shape[1] - 2 * aw, rope, seq, 3)
            rows =lambda p: p.reshape(1, aw).astype(F32)
            h_fwd = None
            for direction in (0, 1):
                h_fwd = rglru_direction(
                    z_a, ev_conv_w[j].astype(F32), rows(ev_conv_b[j]),
                    gate_group_weights(ev_w_r[j, direction], ev_w_i[j, direction]),
                    rows(ev_b_r[j, direction]), rows(ev_b_i[j, direction]),
                    rows(ev_lam[j, direction]), h_fwd, reverse=direction == 1)
            y_a = h_fwd.reshape(t, aw)
            y_b = dilated_mixture(qkv.reshape(bsz, seq, -1)).reshape(t, -1)
            ys, w_out = [y_a, y_b], ev_w_out[j].astype(BF16)
        else:
            lam_init = 0.8 - 0.6 * math.exp(-0.3 * layer)
            w_in = od_w_in[j].astype(BF16)
            qk = qkv_proj(xf, g_pre, w_in, 0, 2 * d, rope, seq, 2)
            vt = norm_matmul_t(xf, g_pre, w_in, 2 * d, d, bsz, seq, _pick(seq, (DIFF_TK, LANES)))
            y = diff_attention(qk.reshape(bsz, seq, -1), vt, od_lam_q1[j], od_lam_k1[j],
                               od_lam_q2[j], od_lam_k2[j], od_subln[j], lam_init)
            ys, w_out = [y.reshape(t, -1)], od_w_out[j].astype(BF16)
        xf = out_proj_residual(ys, w_out, xf, gain(norm_mix_post[layer]))
        xf = ffn_residual(xf, gain(norm_ffn_pre[layer]), gain(norm_ffn_post[layer]),
                          ffn_w_gate[layer].astype(BF16), ffn_w_up[layer].astype(BF16),
                          ffn_w_down[layer].astype(BF16))
    return xf.reshape(bsz, seq, d)
```

```python
import functools
import math

import jax
import jax.numpy as jnp
from jax import lax
from jax.experimental import pallas as pl
from jax.experimental.pallas import tpu as pltpu

F32 = jnp.float32
BF16 = jnp.bfloat16

EPS = 1e-6
NEG = -1e30
HEAD_DIM = 64
ROT_DIM = HEAD_DIM // 4
ROT_HALF = ROT_DIM // 2
ROPE_THETA = 500000.0
CONV_WIDTH = 4
RG_C = 8.0
DILATION_PAIRS = ((128, 1), (512, 4), (2048, 16))
GELU_C = math.sqrt(2.0 / math.pi)

LANES = 128
SUBLANES = 8
MXU_DIM = 256
VMEM_LIMIT_BYTES = 56 * 1024 * 1024

DIFF_TK = 256
DIFF_UNROLL = 16
BF16_SUBLANES = 16
ONES_ROWS = BF16_SUBLANES

LOG2E = math.log2(math.e)
Q_SCALE = HEAD_DIM ** -0.5 * LOG2E

_NT = (((1,), (1,)), ((), ()))


def _pick(n, prefs):
    for p in prefs:
        if n % p == 0:
            return p
    return n


def _cparams(sem):
    return pltpu.CompilerParams(dimension_semantics=sem, vmem_limit_bytes=VMEM_LIMIT_BYTES)


def _rms(x, g):
    return x * lax.rsqrt(jnp.mean(x * x, axis=-1, keepdims=True) + EPS) * g


def _norm_matmul_kernel(x_ref, g_ref, w_ref, o_ref):
    h = _rms(x_ref[...], g_ref[...]).astype(BF16)
    o_ref[...] = jnp.dot(h, w_ref[...], preferred_element_type=F32).astype(o_ref.dtype)


def _col_block(col0, tn):
    assert col0 % tn == 0
    return col0 // tn


def norm_matmul(x, g, w, col0, n, out_dtype):
    t, d = x.shape
    tm = _pick(t, (512, 256, 128))
    tn = _pick(n, (2048, 1024, 512, 256, 128))
    j0 = _col_block(col0, tn)
    return pl.pallas_call(
        _norm_matmul_kernel,
        grid=(t // tm, n // tn),
        in_specs=[pl.BlockSpec((tm, d), lambda i, j: (i, 0)),
                  pl.BlockSpec((1, d), lambda i, j: (0, 0)),
                  pl.BlockSpec((d, tn), lambda i, j: (0, j0 + j))],
        out_specs=pl.BlockSpec((tm, tn), lambda i, j: (i, j)),
        out_shape=jax.ShapeDtypeStruct((t, n), out_dtype),
        compiler_params=_cparams(("parallel", "parallel")),
        name="norm_matmul",
    )(x, g, w)


def _qkv_kernel(x_ref, g_ref, w_ref, c_ref, sa_ref, sb_ref, o_ref):
    h = _rms(x_ref[...], g_ref[...]).astype(BF16)
    acc = jnp.dot(h, w_ref[...], preferred_element_type=F32)
    c, sa, sb = c_ref[...], sa_ref[...], sb_ref[...]
    for cb in range(acc.shape[1] // LANES):
        a = acc[:, cb * LANES:(cb + 1) * LANES]
        r = a * c + pltpu.roll(a, LANES - ROT_HALF, 1) * sa + pltpu.roll(a, ROT_HALF, 1) * sb
        o_ref[:, cb * LANES:(cb + 1) * LANES] = r.astype(o_ref.dtype)


def qkv_proj(x, g, w, col0, n, rope, seq, n_parts):
    t, d = x.shape
    width = n // n_parts
    tm = _pick(seq, (512, 256, 128))
    tn = _pick(width, (2048, 1024, 512, 256, 128))
    per_part = width // tn
    j0 = _col_block(col0, tn)
    nblk_s = seq // tm
    tab_spec = pl.BlockSpec((None, tm, LANES), lambda i, j: (j // per_part, i % nblk_s, 0))
    return pl.pallas_call(
        _qkv_kernel,
        grid=(t // tm, n // tn),
        in_specs=[pl.BlockSpec((tm, d), lambda i, j: (i, 0)),
                  pl.BlockSpec((1, d), lambda i, j: (0, 0)),
                  pl.BlockSpec((d, tn), lambda i, j: (0, j0 + j)),
                  tab_spec, tab_spec, tab_spec],
        out_specs=pl.BlockSpec((tm, tn), lambda i, j: (i, j)),
        out_shape=jax.ShapeDtypeStruct((t, n), BF16),
        compiler_params=_cparams(("parallel", "parallel")),
        name="qkv_proj",
    )(x, g, w, *rope)


def rope_tables(seq):
    pos = jnp.arange(seq, dtype=F32)
    inv = ROPE_THETA ** (-jnp.arange(0, ROT_DIM, 2, dtype=F32) / ROT_DIM)
    ang = pos[:, None] * inv[None, :]
    cos, sin = jnp.cos(ang), jnp.sin(ang)
    zeros = jnp.zeros((seq, HEAD_DIM - ROT_DIM), F32)
    zh = jnp.zeros((seq, ROT_HALF), F32)
    c = jnp.concatenate([cos, cos, zeros + 1.0], axis=1)
    sa = jnp.concatenate([-sin, zh, zeros], axis=1)
    sb = jnp.concatenate([zh, sin, zeros], axis=1)
    rep = LANES // HEAD_DIM
    ident = (jnp.ones_like(c), jnp.zeros_like(c), jnp.zeros_like(c))
    return tuple(jnp.tile(jnp.stack([tab * Q_SCALE, tab, idt]), (1, 1, rep))
                 for tab, idt in zip((c, sa, sb), ident))


def _rglru_kernel(*refs, reverse, final, n_blk):
    if final:
        (xp_ref, xc_ref, xn_ref, cw_ref, cb_ref, wg_ref, br_ref, bi_ref, lam_ref,
         gate_ref, prev_ref, o_ref, carry_ref) = refs
    else:
        (xp_ref, xc_ref, xn_ref, cw_ref, cb_ref, wg_ref, br_ref, bi_ref, lam_ref,
         o_ref, carry_ref) = refs
    i = pl.program_id(1)
    blk = n_blk - 1 - i if reverse else i
    tb, aw = xc_ref.shape

    @pl.when(i == 0)
    def _():
        carry_ref[...] = jnp.zeros_like(carry_ref)

    x0 = xc_ref[...]
    xp = jnp.where(blk > 0, xp_ref[...], 0.0)
    xn = jnp.where(blk < n_blk - 1, xn_ref[...], 0.0)
    row = lax.broadcasted_iota(jnp.int32, (tb, aw), 0)
    zpad = jnp.zeros((tb - SUBLANES, aw), F32)
    conv = cb_ref[...] + cw_ref[2:3, :] * x0
    for back in (1, 2):
        head = jnp.concatenate([pltpu.roll(xp, back, 0), zpad], axis=0)
        shifted = jnp.where(row < back, head, pltpu.roll(x0, back, 0))
        conv = conv + cw_ref[2 - back:3 - back, :] * shifted
    tail = jnp.concatenate([zpad, pltpu.roll(xn, SUBLANES - 1, 0)], axis=0)
    shifted = jnp.where(row == tb - 1, tail, pltpu.roll(x0, tb - 1, 0))
    conv = conv + cw_ref[3:4, :] * shifted

    n_grp, gw, _ = wg_ref.shape
    cbf = conv.astype(BF16)
    zr, zi = [], []
    for g in range(n_grp):
        z = jnp.dot(cbf[:, g * gw:(g + 1) * gw], wg_ref[g], preferred_element_type=F32)
        zr.append(z[:, :gw])
        zi.append(z[:, gw:])
    zr = jnp.concatenate(zr, axis=1) if n_grp > 1 else zr[0]
    zi = jnp.concatenate(zi, axis=1) if n_grp > 1 else zi[0]
    r = 1.0 / (1.0 + jnp.exp(-(zr + br_ref[...])))
    ig = 1.0 / (1.0 + jnp.exp(-(zi + bi_ref[...])))
    nl = -lam_ref[...]
    softplus = jnp.maximum(nl, 0.0) + jnp.log(1.0 + jnp.exp(-jnp.abs(nl)))
    a = jnp.exp(-RG_C * r * softplus)
    u = jnp.sqrt(jnp.maximum(1.0 - a * a, 0.0)) * (ig * conv)

    n_grp8 = tb // SUBLANES
    a3 = a.reshape(n_grp8, SUBLANES, aw)
    u3 = u.reshape(n_grp8, SUBLANES, aw)
    sub = lax.broadcasted_iota(jnp.int32, a3.shape, 1)
    s = 1
    while s < SUBLANES:
        if reverse:
            keep = sub < SUBLANES - s
            a_sh = jnp.where(keep, pltpu.roll(a3, SUBLANES - s, 1), 1.0)
            u_sh = jnp.where(keep, pltpu.roll(u3, SUBLANES - s, 1), 0.0)
        else:
            keep = sub >= s
            a_sh = jnp.where(keep, pltpu.roll(a3, s, 1), 1.0)
            u_sh = jnp.where(keep, pltpu.roll(u3, s, 1), 0.0)
        u3 = a3 * u_sh + u3
        a3 = a3 * a_sh
        s *= 2
    edge = 0 if reverse else SUBLANES - 1
    state = carry_ref[edge:edge + 1, :]
    groups = [None] * n_grp8
    for j in (range(n_grp8 - 1, -1, -1) if reverse else range(n_grp8)):
        groups[j] = a3[j] * state + u3[j]
        state = groups[j][edge:edge + 1, :]
    h = jnp.concatenate(groups, axis=0)
    carry_ref[...] = groups[0] if reverse else groups[-1]

    if final:
        gt = gate_ref[...]
        gelu = 0.5 * gt * (1.0 + jnp.tanh(GELU_C * (gt + 0.044715 * (gt * gt * gt))))
        o_ref[...] = ((prev_ref[...] + h) * gelu).astype(o_ref.dtype)
    else:
        o_ref[...] = h


def rglru_direction(z_a, conv_w, conv_b, w_gate, b_r, b_i, lam, prev, reverse):
    bsz, seq, two_aw = z_a.shape
    aw = two_aw // 2
    tb = _pick(seq, (256, 128, 64, 32, 16, 8))
    n_blk = seq // tb
    rows8 = seq // SUBLANES
    per8 = tb // SUBLANES
    final = prev is not None

    def blk(i):
        return n_blk - 1 - i if reverse else i

    xr_specs = [
        pl.BlockSpec((None, SUBLANES, aw), lambda b, i: (b, jnp.maximum(blk(i) * per8 - 1, 0), 1)),
        pl.BlockSpec((None, tb, aw), lambda b, i: (b, blk(i), 1)),
        pl.BlockSpec((None, SUBLANES, aw),
                     lambda b, i: (b, jnp.minimum((blk(i) + 1) * per8, rows8 - 1), 1)),
    ]
    full = lambda a: pl.BlockSpec(a.shape, lambda b, i: (0,) * a.ndim)
    params = [conv_w, conv_b, w_gate, b_r, b_i, lam]
    in_specs = xr_specs + [full(p) for p in params]
    args = [z_a, z_a, z_a] + params
    if final:
        in_specs += [pl.BlockSpec((None, tb, aw), lambda b, i: (b, blk(i), 0)),
                     pl.BlockSpec((None, tb, aw), lambda b, i: (b, blk(i), 0))]
        args += [z_a, prev]
    return pl.pallas_call(
        functools.partial(_rglru_kernel, reverse=reverse, final=final, n_blk=n_blk),
        grid=(bsz, n_blk),
        in_specs=in_specs,
        out_specs=pl.BlockSpec((None, tb, aw), lambda b, i: (b, blk(i), 0)),
        out_shape=jax.ShapeDtypeStruct((bsz, seq, aw), BF16 if final else F32),
        scratch_shapes=[pltpu.VMEM((SUBLANES, aw), F32)],
        compiler_params=_cparams(("parallel", "arbitrary")),
        name="rglru_bwd" if reverse else "rglru_fwd",
    )(*args)


def gate_group_weights(w_r, w_i):
    n, bs, _ = w_r.shape
    aw = n * bs
    gw = _pick(aw, (MXU_DIM, LANES))
    per = gw // bs
    n_grp = aw // gw
    eye = jnp.eye(per, dtype=w_r.dtype)

    def dense(w):
        wg = w.reshape(n_grp, per, bs, bs)
        return jnp.einsum('gaef,ab->gaebf', wg, eye).reshape(n_grp, gw, gw)

    return jnp.concatenate([dense(w_r), dense(w_i)], axis=2).astype(BF16)


DIL_TILE = 128
DIL_UNROLL = 8


def _dilated_kernel(q_ref, kp_ref, kc_ref, kn_ref, vp_ref, vc_ref, vn_ref, o_ref,
                    qbuf, kbuf, vbuf, so_ref, sl_ref, *, seq, half, dilations):
    tb = q_ref.shape[0]
    halo = kp_ref.shape[0]
    t0 = pl.program_id(1) * tb
    qbuf[...] = q_ref[...].astype(F32)
    for buf, prev, cur, nxt in ((kbuf, kp_ref, kc_ref, kn_ref), (vbuf, vp_ref, vc_ref, vn_ref)):
        buf[0:halo] = prev[...].astype(F32)
        buf[halo:halo + tb] = cur[...].astype(F32)
        buf[halo + tb:halo + tb + halo] = nxt[...].astype(F32)

    nk = DIL_TILE + 2 * half
    rowq = lax.broadcasted_iota(jnp.int32, (DIL_TILE, nk), 0)
    colk = lax.broadcasted_iota(jnp.int32, (DIL_TILE, nk), 1)
    band_bias = jnp.where(jnp.abs(colk - half - rowq) <= half, 0.0, NEG).astype(F32)
    col1 = lax.broadcasted_iota(jnp.int32, (1, nk), 1)
    lane = lax.broadcasted_iota(jnp.int32, (DIL_TILE, LANES), 1)
    lo = lane < HEAD_DIM

    for bi, d in enumerate(dilations):
        span = DIL_TILE * d

        def rows(ref, start, size, d=d):
            if d == 1:
                return ref[pl.ds(pl.multiple_of(start, SUBLANES), size), :]
            return ref[pl.ds(start, size, stride=d), :]

        def tile(n, carry, bi=bi, d=d, span=span, rows=rows):
            g, r = n // d, n % d
            qs = g * span + r
            ks = halo + qs - half * d
            tok = (t0 - halo + ks) + col1 * d
            bias = band_bias + jnp.where((tok >= 0) & (tok < seq), 0.0, NEG)
            q = rows(qbuf, qs, DIL_TILE).astype(BF16)
            k = rows(kbuf, ks, nk).astype(BF16)
            v = rows(vbuf, ks, nk).astype(BF16)
            outs, lses = [], []
            for first in (True, False):
                qm = jnp.where(lo if first else ~lo, q, jnp.zeros_like(q))
                s = lax.dot_general(qm, k, _NT, preferred_element_type=F32) + bias
                m = jnp.max(s, axis=1, keepdims=True)
                p = jnp.exp2(s - m)
                den = jnp.sum(p, axis=1, keepdims=True)
                pv = jnp.dot(p.astype(BF16), v, preferred_element_type=F32)
                outs.append(pv / den)
                lses.append(m + jnp.log2(den))
            o = jnp.where(lo, outs[0], outs[1])
            lse = jnp.where(lo, lses[0], lses[1])
            if d == 1:
                dst = pl.ds(pl.multiple_of(qs, SUBLANES), DIL_TILE)
            else:
                dst = pl.ds(qs, DIL_TILE, stride=d)
            if bi > 0:
                lp = sl_ref[dst, :]
                mx = jnp.maximum(lp, lse)
                tot = mx + jnp.log2(jnp.exp2(lp - mx) + jnp.exp2(lse - mx))
                o = so_ref[dst, :] * jnp.exp2(lp - tot) + o * jnp.exp2(lse - tot)
                lse = tot
            so_ref[dst, :] = o
            if bi < len(dilations) - 1:
                sl_ref[dst, :] = lse
            return carry

        lax.fori_loop(0, tb // DIL_TILE, tile, 0, unroll=DIL_UNROLL)

    o_ref[...] = so_ref[...].astype(o_ref.dtype)


def dilated_mixture(qkv):
    bsz, seq, three_w = qkv.shape
    width = three_w // 3
    n_pairs = width // LANES
    dilations = tuple(d for _, d in DILATION_PAIRS)
    halves = {w // (2 * d) for w, d in DILATION_PAIRS}
    assert len(halves) == 1
    half = halves.pop()
    halo = half * max(dilations)
    tb = DIL_TILE * max(dilations)
    assert seq % tb == 0 and tb % halo == 0 and all(max(dilations) % d == 0 for d in dilations)
    per = tb // halo
    n_halo = seq // halo

    def centre(col):
        return pl.BlockSpec((None, tb, LANES), lambda b, i, p: (b, i, col * n_pairs + p))

    def before(col):
        return pl.BlockSpec((None, halo, LANES),
                            lambda b, i, p: (b, jnp.maximum(i * per - 1, 0), col * n_pairs + p))

    def after(col):
        return pl.BlockSpec((None, halo, LANES),
                            lambda b, i, p: (b, jnp.minimum((i + 1) * per, n_halo - 1),
                                             col * n_pairs + p))

    return pl.pallas_call(
        functools.partial(_dilated_kernel, seq=seq, half=half, dilations=dilations),
        grid=(bsz, seq // tb, n_pairs),
        in_specs=[centre(0), before(1), centre(1), after(1), before(2), centre(2), after(2)],
        out_specs=pl.BlockSpec((None, tb, LANES), lambda b, i, p: (b, i, p)),
        out_shape=jax.ShapeDtypeStruct((bsz, seq, width), BF16),
        scratch_shapes=[pltpu.VMEM((tb, LANES), F32),
                        pltpu.VMEM((tb + 2 * halo, LANES), F32),
                        pltpu.VMEM((tb + 2 * halo, LANES), F32),
                        pltpu.VMEM((tb, LANES), F32),
                        pltpu.VMEM((tb, LANES), F32)],
        compiler_params=_cparams(("parallel", "parallel", "parallel")),
        name="dilated_mixture",
    )(*([qkv] * 7))


def _norm_matmul_t_kernel(x_ref, g_ref, w_ref, o_ref):
    h = _rms(x_ref[...], g_ref[...]).astype(BF16)
    acc_t = jnp.dot(h, w_ref[...], preferred_element_type=F32).T
    n_chunk, _, tk = o_ref.shape
    for c in range(n_chunk):
        o_ref[c] = acc_t[:, c * tk:(c + 1) * tk].astype(o_ref.dtype)


def norm_matmul_t(x, g, w, col0, n, bsz, seq, tk):
    t, d = x.shape
    tm = _pick(seq, (512, 256, 128))
    tn = _pick(n, (1024, 512, 256, 128))
    j0 = _col_block(col0, tn)
    assert tm % tk == 0
    per = tm // tk
    nblk_s = seq // tm
    return pl.pallas_call(
        _norm_matmul_t_kernel,
        grid=(t // tm, n // tn),
        in_specs=[pl.BlockSpec((tm, d), lambda i, j: (i, 0)),
                  pl.BlockSpec((1, d), lambda i, j: (0, 0)),
                  pl.BlockSpec((d, tn), lambda i, j: (0, j0 + j))],
        out_specs=pl.BlockSpec((None, per, tn, tk), lambda i, j: (i // nblk_s, i % nblk_s, j, 0)),
        out_shape=jax.ShapeDtypeStruct((bsz, seq // tk, n, tk), BF16),
        compiler_params=_cparams(("parallel", "parallel")),
        name="norm_matmul_t",
    )(x, g, w)


def _diff_attn_kernel(q_ref, k_ref, vt_ref, lq1_ref, lk1_ref, lq2_ref, lk2_ref, sub_ref, o_ref,
                      sa_ref, sb_ref, acc_ref, *, lam_init):
    tq = q_ref.shape[0]
    n_kt, _, tk = vt_ref.shape
    q = q_ref[...]
    lane = lax.broadcasted_iota(jnp.int32, (tq, LANES), 1)
    lo = lane < HEAD_DIM
    q1 = jnp.where(lo, q, jnp.zeros_like(q))
    q2 = jnp.where(lo, jnp.zeros_like(q), q)

    def scores(kt, s_ref):
        k = k_ref[pl.ds(pl.multiple_of(kt * tk, tk), tk), :]
        cms = []
        for j, qm in enumerate((q1, q2)):
            st = lax.dot_general(k, qm, _NT, preferred_element_type=F32)
            s_ref[j] = st
            cms.append(jnp.max(st, axis=0, keepdims=True))
        return tuple(cms)

    ones = jnp.ones((ONES_ROWS, tk), BF16)

    def absorb(kt, s_ref, ms, cms):
        vt = jnp.concatenate([vt_ref[kt], ones], axis=0)
        out = []
        for j in range(2):
            m_new = jnp.maximum(ms[j], cms[j])
            alpha = jnp.exp2(ms[j] - m_new)
            p = jnp.exp2((s_ref[j] - m_new).astype(BF16))
            acc_ref[j] = alpha * acc_ref[j] + jnp.dot(vt, p, preferred_element_type=F32)
            out.append(m_new)
        return tuple(out)

    def body(it, carry):
        ms, cms = carry[:2], carry[2:]
        kt = DIFF_UNROLL * it
        bufs = (sa_ref, sb_ref)
        for u in range(DIFF_UNROLL):
            cms_next = scores(jnp.minimum(kt + u + 1, n_kt - 1), bufs[(u + 1) % 2])
            ms = absorb(kt + u, bufs[u % 2], ms, cms)
            cms = cms_next
        return ms + cms

    m0 = jnp.full((1, tq), NEG, F32)
    acc_ref[...] = jnp.zeros_like(acc_ref)
    lax.fori_loop(0, n_kt // DIFF_UNROLL, body, (m0, m0) + scores(0, sa_ref))
    a1, l1 = acc_ref[0, :LANES], acc_ref[0, LANES:LANES + 1]
    a2, l2 = acc_ref[1, :LANES], acc_ref[1, LANES:LANES + 1]

    lam = (jnp.exp(jnp.sum(lq1_ref[...] * lk1_ref[...], axis=1, keepdims=True))
           - jnp.exp(jnp.sum(lq2_ref[...] * lk2_ref[...], axis=1, keepdims=True)) + lam_init)
    o_t = a1 * (1.0 / l1) - lam * (a2 * (1.0 / l2))
    o_t = o_t * (lax.rsqrt(jnp.mean(o_t * o_t, axis=0, keepdims=True) + EPS) * (1.0 - lam_init))
    o_ref[...] = (o_t.T * sub_ref[...]).astype(o_ref.dtype)


def diff_attention(qk, vt, lq1, lk1, lq2, lk2, subln, lam_init):
    bsz, seq, two_d = qk.shape
    d = two_d // 2
    n_heads = d // LANES
    n_kt, tk = vt.shape[1], vt.shape[3]
    assert DIFF_UNROLL % 2 == 0 and n_kt % DIFF_UNROLL == 0
    tq = _pick(seq, (512, 256, 128))
    row = lambda a: a.reshape(1, -1).astype(F32)
    small = lambda n: pl.BlockSpec((1, n), lambda b, h, i: (0, 0))
    return pl.pallas_call(
        functools.partial(_diff_attn_kernel, lam_init=lam_init),
        grid=(bsz, n_heads, seq // tq),
        in_specs=[pl.BlockSpec((None, tq, LANES), lambda b, h, i: (b, i, h)),
                  pl.BlockSpec((None, seq, LANES), lambda b, h, i: (b, 0, n_heads + h)),
                  pl.BlockSpec((None, n_kt, LANES, tk), lambda b, h, i: (b, 0, h, 0)),
                  small(HEAD_DIM), small(HEAD_DIM), small(HEAD_DIM), small(HEAD_DIM),
                  small(LANES)],
        out_specs=pl.BlockSpec((None, tq, LANES), lambda b, h, i: (b, i, h)),
        out_shape=jax.ShapeDtypeStruct((bsz, seq, d), BF16),
        scratch_shapes=[pltpu.VMEM((2, tk, tq), F32), pltpu.VMEM((2, tk, tq), F32),
                        pltpu.VMEM((2, LANES + ONES_ROWS, tq), F32)],
        compiler_params=_cparams(("parallel", "parallel", "parallel")),
        name="diff_attention",
    )(qk, qk, vt, row(lq1), row(lk1), row(lq2), row(lk2), row(subln))


def _out_proj_kernel(*refs, n_in):
    ys = refs[:n_in]
    ws = refs[n_in:2 * n_in]
    x_ref, g_ref, o_ref = refs[2 * n_in:]
    m = jnp.dot(ys[0][...], ws[0][...], preferred_element_type=F32)
    for y_ref, w_ref in zip(ys[1:], ws[1:]):
        m = m + jnp.dot(y_ref[...], w_ref[...], preferred_element_type=F32)
    o_ref[...] = x_ref[...] + _rms(m, g_ref[...])


def out_proj_residual(ys, w, x, g):
    t, d = x.shape
    tm = _pick(t, (512, 256, 128))
    kdim = ys[0].shape[1]
    assert all(y.shape[1] == kdim for y in ys) and w.shape[0] == kdim * len(ys)
    in_specs = ([pl.BlockSpec((tm, kdim), lambda i: (i, 0)) for _ in ys]
                + [pl.BlockSpec((kdim, d), lambda i, r=r: (r, 0)) for r in range(len(ys))]
                + [pl.BlockSpec((tm, d), lambda i: (i, 0)), pl.BlockSpec((1, d), lambda i: (0, 0))])
    return pl.pallas_call(
        functools.partial(_out_proj_kernel, n_in=len(ys)),
        grid=(t // tm,),
        in_specs=in_specs,
        out_specs=pl.BlockSpec((tm, d), lambda i: (i, 0)),
        out_shape=jax.ShapeDtypeStruct((t, d), F32),
        compiler_params=_cparams(("parallel",)),
        name="out_proj",
    )(*ys, *([w] * len(ys)), x, g)


def _ffn_kernel(x_ref, gpre_ref, gpost_ref, wg_ref, wu_ref, wd_ref, o_ref, h_ref, acc_ref):
    f = pl.program_id(1)

    @pl.when(f == 0)
    def _():
        h_ref[...] = _rms(x_ref[...], gpre_ref[...]).astype(BF16)
        acc_ref[...] = jnp.zeros_like(acc_ref)

    h = h_ref[...]
    gate = jnp.dot(h, wg_ref[...], preferred_element_type=F32)
    up = jnp.dot(h, wu_ref[...], preferred_element_type=F32)
    act = (gate / (1.0 + jnp.exp(-gate))) * up
    acc_ref[...] += jnp.dot(act.astype(BF16), wd_ref[...], preferred_element_type=F32)

    @pl.when(f == pl.num_programs(1) - 1)
    def _():
        o_ref[...] = x_ref[...] + _rms(acc_ref[...], gpost_ref[...])


def ffn_residual(x, g_pre, g_post, w_gate, w_up, w_down):
    t, d = x.shape
    f = w_gate.shape[1]
    tm = _pick(t, (512, 256, 128))
    tf = _pick(f, (512, 256, 128))
    return pl.pallas_call(
        _ffn_kernel,
        grid=(t // tm, f // tf),
        in_specs=[pl.BlockSpec((tm, d), lambda i, j: (i, 0)),
                  pl.BlockSpec((1, d), lambda i, j: (0, 0)),
                  pl.BlockSpec((1, d), lambda i, j: (0, 0)),
                  pl.BlockSpec((d, tf), lambda i, j: (0, j)),
                  pl.BlockSpec((d, tf), lambda i, j: (0, j)),
                  pl.BlockSpec((tf, d), lambda i, j: (j, 0))],
        out_specs=pl.BlockSpec((tm, d), lambda i, j: (i, 0)),
        out_shape=jax.ShapeDtypeStruct((t, d), F32),
        scratch_shapes=[pltpu.VMEM((tm, d), BF16), pltpu.VMEM((tm, d), F32)],
        compiler_params=_cparams(("parallel", "arbitrary")),
        name="ffn",
    )(x, g_pre, g_post, w_gate, w_up, w_down)


def kernel(x, norm_mix_pre, norm_mix_post, norm_ffn_pre, norm_ffn_post, ev_w_in, ev_conv_w, ev_conv_b, ev_w_r, ev_b_r, ev_w_i, ev_b_i, ev_lam, ev_w_out, od_w_in, od_lam_q1, od_lam_k1, od_lam_q2, od_lam_k2, od_subln, od_w_out, ffn_w_gate, ffn_w_up, ffn_w_down):
    bsz, seq, d = x.shape
    depth = norm_mix_pre.shape[0]
    t = bsz * seq
    aw = ev_conv_w.shape[-1]
    rope = rope_tables(seq)
    gain = lambda g: g.reshape(1, d).astype(F32)
    xf = x.reshape(t, d)
    for layer in range(depth):
        j = layer // 2
        g_pre = gain(norm_mix_pre[layer])
        if layer % 2 == 0:
            w_in = ev_w_in[j].astype(BF16)
            z_a = norm_matmul(xf, g_pre, w_in, 0, 2 * aw, F32).reshape(bsz, seq, 2 * aw)
            qkv = qkv_proj(xf, g_pre, w_in, 2 * aw, w_in.shape[1] - 2 * aw, rope, seq, 3)
            rows = lambda p: p.reshape(1, aw).astype(F32)
            h_fwd = None
            for direction in (0, 1):
                h_fwd = rglru_direction(
                    z_a, ev_conv_w[j].astype(F32), rows(ev_conv_b[j]),
                    gate_group_weights(ev_w_r[j, direction], ev_w_i[j, direction]),
                    rows(ev_b_r[j, direction]), rows(ev_b_i[j, direction]),
                    rows(ev_lam[j, direction]), h_fwd, reverse=direction == 1)
            y_a = h_fwd.reshape(t, aw)
            y_b = dilated_mixture(qkv.reshape(bsz, seq, -1)).reshape(t, -1)
            ys, w_out = [y_a, y_b], ev_w_out[j].astype(BF16)
        else:
            lam_init = 0.8 - 0.6 * math.exp(-0.3 * layer)
            w_in = od_w_in[j].astype(BF16)
            qk = qkv_proj(xf, g_pre, w_in, 0, 2 * d, rope, seq, 2)
            vt = norm_matmul_t(xf, g_pre, w_in, 2 * d, d, bsz, seq, _pick(seq, (DIFF_TK, LANES)))
            y = diff_attention(qk.reshape(bsz, seq, -1), vt, od_lam_q1[j], od_lam_k1[j],
                               od_lam_q2[j], od_lam_k2[j], od_subln[j], lam_init)
            ys, w_out = [y.reshape(t, -1)], od_w_out[j].astype(BF16)
        xf = out_proj_residual(ys, w_out, xf, gain(norm_mix_post[layer]))
        xf = ffn_residual(xf, gain(norm_ffn_pre[layer]), gain(norm_ffn_post[layer]),
                          ffn_w_gate[layer].astype(BF16), ffn_w_up[layer].astype(BF16),
                          ffn_w_down[layer].astype(BF16))
    return xf.reshape(bsz, seq, d)
```

```python
import functools
import math

import jax
import jax.numpy as jnp
from jax import lax
from jax.experimental import pallas as pl
from jax.experimental.pallas import tpu as pltpu

F32 = jnp.float32
BF16 = jnp.bfloat16

EPS = 1e-6
NEG = -1e30
HEAD_DIM = 64
ROT_DIM = HEAD_DIM // 4
ROT_HALF = ROT_DIM // 2
ROPE_THETA = 500000.0
CONV_WIDTH = 4
RG_C = 8.0
DILATION_PAIRS = ((128, 1), (512, 4), (2048, 16))
GELU_C = math.sqrt(2.0 / math.pi)

LANES = 128
SUBLANES = 8
MXU_DIM = 256
VMEM_LIMIT_BYTES = 56 * 1024 * 1024

DIFF_TK = 256
DIFF_UNROLL = 32
BF16_SUBLANES = 16
ONES_ROWS = BF16_SUBLANES

LOG2E = math.log2(math.e)
Q_SCALE = HEAD_DIM ** -0.5 * LOG2E

_NT = (((1,), (1,)), ((), ()))


def _pick(n, prefs):
    for p in prefs:
        if n % p == 0:
            return p
    return n


def _cparams(sem):
    return pltpu.CompilerParams(dimension_semantics=sem, vmem_limit_bytes=VMEM_LIMIT_BYTES)


def _rms(x, g):
    return x * lax.rsqrt(jnp.mean(x * x, axis=-1, keepdims=True) + EPS) * g


def _norm_matmul_kernel(x_ref, g_ref, w_ref, o_ref):
    h = _rms(x_ref[...], g_ref[...]).astype(BF16)
    o_ref[...] = jnp.dot(h, w_ref[...], preferred_element_type=F32).astype(o_ref.dtype)


def _col_block(col0, tn):
    assert col0 % tn == 0
    return col0 // tn


def norm_matmul(x, g, w, col0, n, out_dtype):
    t, d = x.shape
    tm = _pick(t, (512, 256, 128))
    tn = _pick(n, (2048, 1024, 512, 256, 128))
    j0 = _col_block(col0, tn)
    return pl.pallas_call(
        _norm_matmul_kernel,
        grid=(t // tm, n // tn),
        in_specs=[pl.BlockSpec((tm, d), lambda i, j: (i, 0)),
                  pl.BlockSpec((1, d), lambda i, j: (0, 0)),
                  pl.BlockSpec((d, tn), lambda i, j: (0, j0 + j))],
        out_specs=pl.BlockSpec((tm, tn), lambda i, j: (i, j)),
        out_shape=jax.ShapeDtypeStruct((t, n), out_dtype),
        compiler_params=_cparams(("parallel", "parallel")),
        name="norm_matmul",
    )(x, g, w)


def _qkv_kernel(x_ref, g_ref, w_ref, c_ref, sa_ref, sb_ref, o_ref):
    h = _rms(x_ref[...], g_ref[...]).astype(BF16)
    acc = jnp.dot(h, w_ref[...], preferred_element_type=F32)
    c, sa, sb = c_ref[...], sa_ref[...], sb_ref[...]
    for cb in range(acc.shape[1] // LANES):
        a = acc[:, cb * LANES:(cb + 1) * LANES]
        r = a * c + pltpu.roll(a, LANES - ROT_HALF, 1) * sa + pltpu.roll(a, ROT_HALF, 1) * sb
        o_ref[:, cb * LANES:(cb + 1) * LANES] = r.astype(o_ref.dtype)


def qkv_proj(x, g, w, col0, n, rope, seq, n_parts):
    t, d = x.shape
    width = n // n_parts
    tm = _pick(seq, (512, 256, 128))
    tn = _pick(width, (2048, 1024, 512, 256, 128))
    per_part = width // tn
    j0 = _col_block(col0, tn)
    nblk_s = seq // tm
    tab_spec = pl.BlockSpec((None, tm, LANES), lambda i, j: (j // per_part, i % nblk_s, 0))
    return pl.pallas_call(
        _qkv_kernel,
        grid=(t // tm, n // tn),
        in_specs=[pl.BlockSpec((tm, d), lambda i, j: (i, 0)),
                  pl.BlockSpec((1, d), lambda i, j: (0, 0)),
                  pl.BlockSpec((d, tn), lambda i, j: (0, j0 + j)),
                  tab_spec, tab_spec, tab_spec],
        out_specs=pl.BlockSpec((tm, tn), lambda i, j: (i, j)),
        out_shape=jax.ShapeDtypeStruct((t, n), BF16),
        compiler_params=_cparams(("parallel", "parallel")),
        name="qkv_proj",
    )(x, g, w, *rope)


def rope_tables(seq):
    pos = jnp.arange(seq, dtype=F32)
    inv = ROPE_THETA ** (-jnp.arange(0, ROT_DIM, 2, dtype=F32) / ROT_DIM)
    ang = pos[:, None] * inv[None, :]
    cos, sin = jnp.cos(ang), jnp.sin(ang)
    zeros = jnp.zeros((seq, HEAD_DIM - ROT_DIM), F32)
    zh = jnp.zeros((seq, ROT_HALF), F32)
    c = jnp.concatenate([cos, cos, zeros + 1.0], axis=1)
    sa = jnp.concatenate([-sin, zh, zeros], axis=1)
    sb = jnp.concatenate([zh, sin, zeros], axis=1)
    rep = LANES // HEAD_DIM
    ident = (jnp.ones_like(c), jnp.zeros_like(c), jnp.zeros_like(c))
    return tuple(jnp.tile(jnp.stack([tab * Q_SCALE, tab, idt]), (1, 1, rep))
                 for tab, idt in zip((c, sa, sb), ident))


def _rglru_kernel(*refs, reverse, final, n_blk):
    if final:
        (xp_ref, xc_ref, xn_ref, cw_ref, cb_ref, wg_ref, br_ref, bi_ref, lam_ref,
         gate_ref, prev_ref, o_ref, carry_ref) = refs
    else:
        (xp_ref, xc_ref, xn_ref, cw_ref, cb_ref, wg_ref, br_ref, bi_ref, lam_ref,
         o_ref, carry_ref) = refs
    i = pl.program_id(1)
    blk = n_blk - 1 - i if reverse else i
    tb, aw = xc_ref.shape

    @pl.when(i == 0)
    def _():
        carry_ref[...] = jnp.zeros_like(carry_ref)

    x0 = xc_ref[...]
    xp = jnp.where(blk > 0, xp_ref[...], 0.0)
    xn = jnp.where(blk < n_blk - 1, xn_ref[...], 0.0)
    row = lax.broadcasted_iota(jnp.int32, (tb, aw), 0)
    zpad = jnp.zeros((tb - SUBLANES, aw), F32)
    conv = cb_ref[...] + cw_ref[2:3, :] * x0
    for back in (1, 2):
        head = jnp.concatenate([pltpu.roll(xp, back, 0), zpad], axis=0)
        shifted = jnp.where(row < back, head, pltpu.roll(x0, back, 0))
        conv = conv + cw_ref[2 - back:3 - back, :] * shifted
    tail = jnp.concatenate([zpad, pltpu.roll(xn, SUBLANES - 1, 0)], axis=0)
    shifted = jnp.where(row == tb - 1, tail, pltpu.roll(x0, tb - 1, 0))
    conv = conv + cw_ref[3:4, :] * shifted

    n_grp, gw, _ = wg_ref.shape
    cbf = conv.astype(BF16)
    zr, zi = [], []
    for g in range(n_grp):
        z = jnp.dot(cbf[:, g * gw:(g + 1) * gw], wg_ref[g], preferred_element_type=F32)
        zr.append(z[:, :gw])
        zi.append(z[:, gw:])
    zr = jnp.concatenate(zr, axis=1) if n_grp > 1 else zr[0]
    zi = jnp.concatenate(zi, axis=1) if n_grp > 1 else zi[0]
    r = 1.0 / (1.0 + jnp.exp(-(zr + br_ref[...])))
    ig = 1.0 / (1.0 + jnp.exp(-(zi + bi_ref[...])))
    nl = -lam_ref[...]
    softplus = jnp.maximum(nl, 0.0) + jnp.log(1.0 + jnp.exp(-jnp.abs(nl)))
    a = jnp.exp(-RG_C * r * softplus)
    u = jnp.sqrt(jnp.maximum(1.0 - a * a, 0.0)) * (ig * conv)

    n_grp8 = tb // SUBLANES
    a3 = a.reshape(n_grp8, SUBLANES, aw)
    u3 = u.reshape(n_grp8, SUBLANES, aw)
    sub = lax.broadcasted_iota(jnp.int32, a3.shape, 1)
    s = 1
    while s < SUBLANES:
        if reverse:
            keep = sub < SUBLANES - s
            a_sh = jnp.where(keep, pltpu.roll(a3, SUBLANES - s, 1), 1.0)
            u_sh = jnp.where(keep, pltpu.roll(u3, SUBLANES - s, 1), 0.0)
        else:
            keep = sub >= s
            a_sh = jnp.where(keep, pltpu.roll(a3, s, 1), 1.0)
            u_sh = jnp.where(keep, pltpu.roll(u3, s, 1), 0.0)
        u3 = a3 * u_sh + u3
        a3 = a3 * a_sh
        s *= 2
    edge = 0 if reverse else SUBLANES - 1
    state = carry_ref[edge:edge + 1, :]
    groups = [None] * n_grp8
    for j in (range(n_grp8 - 1, -1, -1) if reverse else range(n_grp8)):
        groups[j] = a3[j] * state + u3[j]
        state = groups[j][edge:edge + 1, :]
    h = jnp.concatenate(groups, axis=0)
    carry_ref[...] = groups[0] if reverse else groups[-1]

    if final:
        gt = gate_ref[...]
        gelu = 0.5 * gt * (1.0 + jnp.tanh(GELU_C * (gt + 0.044715 * (gt * gt * gt))))
        o_ref[...] = ((prev_ref[...] + h) * gelu).astype(o_ref.dtype)
    else:
        o_ref[...] = h


def rglru_direction(z_a, conv_w, conv_b, w_gate, b_r, b_i, lam, prev, reverse):
    bsz, seq, two_aw = z_a.shape
    aw = two_aw // 2
    tb = _pick(seq, (256, 128, 64, 32, 16, 8))
    n_blk = seq // tb
    rows8 = seq // SUBLANES
    per8 = tb // SUBLANES
    final = prev is not None

    def blk(i):
        return n_blk - 1 - i if reverse else i

    xr_specs = [
        pl.BlockSpec((None, SUBLANES, aw), lambda b, i: (b, jnp.maximum(blk(i) * per8 - 1, 0), 1)),
        pl.BlockSpec((None, tb, aw), lambda b, i: (b, blk(i), 1)),
        pl.BlockSpec((None, SUBLANES, aw),
                     lambda b, i: (b, jnp.minimum((blk(i) + 1) * per8, rows8 - 1), 1)),
    ]
    full = lambda a: pl.BlockSpec(a.shape, lambda b, i: (0,) * a.ndim)
    params = [conv_w, conv_b, w_gate, b_r, b_i, lam]
    in_specs = xr_specs + [full(p) for p in params]
    args = [z_a, z_a, z_a] + params
    if final:
        in_specs += [pl.BlockSpec((None, tb, aw), lambda b, i: (b, blk(i), 0)),
                     pl.BlockSpec((None, tb, aw), lambda b, i: (b, blk(i), 0))]
        args += [z_a, prev]
    return pl.pallas_call(
        functools.partial(_rglru_kernel, reverse=reverse, final=final, n_blk=n_blk),
        grid=(bsz, n_blk),
        in_specs=in_specs,
        out_specs=pl.BlockSpec((None, tb, aw), lambda b, i: (b, blk(i), 0)),
        out_shape=jax.ShapeDtypeStruct((bsz, seq, aw), BF16 if final else F32),
        scratch_shapes=[pltpu.VMEM((SUBLANES, aw), F32)],
        compiler_params=_cparams(("parallel", "arbitrary")),
        name="rglru_bwd" if reverse else "rglru_fwd",
    )(*args)


def gate_group_weights(w_r, w_i):
    n, bs, _ = w_r.shape
    aw = n * bs
    gw = _pick(aw, (MXU_DIM, LANES))
    per = gw // bs
    n_grp = aw // gw
    eye = jnp.eye(per, dtype=w_r.dtype)

    def dense(w):
        wg = w.reshape(n_grp, per, bs, bs)
        return jnp.einsum('gaef,ab->gaebf', wg, eye).reshape(n_grp, gw, gw)

    return jnp.concatenate([dense(w_r), dense(w_i)], axis=2).astype(BF16)


DIL_TILE = 128
DIL_UNROLL = 16


def _dilated_kernel(q_ref, kp_ref, kc_ref, kn_ref, vp_ref, vc_ref, vn_ref, o_ref,
                    qbuf, kbuf, vbuf, so_ref, sl_ref, *, seq, half, dilations):
    tb = q_ref.shape[0]
    halo = kp_ref.shape[0]
    t0 = pl.program_id(1) * tb
    qbuf[...] = q_ref[...].astype(F32)
    for buf, prev, cur, nxt in ((kbuf, kp_ref, kc_ref, kn_ref), (vbuf, vp_ref, vc_ref, vn_ref)):
        buf[0:halo] = prev[...].astype(F32)
        buf[halo:halo + tb] = cur[...].astype(F32)
        buf[halo + tb:halo + tb + halo] = nxt[...].astype(F32)

    nk = DIL_TILE + 2 * half
    rowq = lax.broadcasted_iota(jnp.int32, (DIL_TILE, nk), 0)
    colk = lax.broadcasted_iota(jnp.int32, (DIL_TILE, nk), 1)
    band_bias = jnp.where(jnp.abs(colk - half - rowq) <= half, 0.0, NEG).astype(F32)
    col1 = lax.broadcasted_iota(jnp.int32, (1, nk), 1)
    lane = lax.broadcasted_iota(jnp.int32, (DIL_TILE, LANES), 1)
    lo = lane < HEAD_DIM
    lo_k = lax.broadcasted_iota(jnp.int32, (nk, LANES), 1) < HEAD_DIM

    for bi, d in enumerate(dilations):
        span = DIL_TILE * d

        def rows(ref, start, size, d=d):
            if d == 1:
                return ref[pl.ds(pl.multiple_of(start, SUBLANES), size), :]
            return ref[pl.ds(start, size, stride=d), :]

        def tile(n, carry, bi=bi, d=d, span=span, rows=rows):
            g, r = n // d, n % d
            qs = g * span + r
            ks = halo + qs - half * d
            tok = (t0 - halo + ks) + col1 * d
            bias = band_bias + jnp.where((tok >= 0) & (tok < seq), 0.0, NEG)
            q = rows(qbuf, qs, DIL_TILE).astype(BF16)
            k = rows(kbuf, ks, nk).astype(BF16)
            v = rows(vbuf, ks, nk).astype(BF16)
            res, ms = None, []
            for first in (True, False):
                own = lo if first else ~lo
                qm = jnp.where(own, q, jnp.zeros_like(q))
                s = lax.dot_general(qm, k, _NT, preferred_element_type=F32) + bias
                m = jnp.max(s, axis=1, keepdims=True)
                p = jnp.exp2((s - m).astype(BF16))
                own_k = lo_k if first else ~lo_k
                v_ext = jnp.concatenate(
                    [jnp.where(own_k, v, jnp.zeros_like(v)),
                     jnp.where(own_k, 1.0, 0.0).astype(BF16)], axis=1)
                pv = jnp.dot(p, v_ext, preferred_element_type=F32)
                res = pv if res is None else res + pv
                ms.append(m)
            den = res[:, LANES:]
            o = res[:, :LANES] / den
            lse = jnp.where(lo, ms[0], ms[1]) + jnp.log2(den)
            if d == 1:
                dst = pl.ds(pl.multiple_of(qs, SUBLANES), DIL_TILE)
            else:
                dst = pl.ds(qs, DIL_TILE, stride=d)
            if bi > 0:
                lp = sl_ref[dst, :]
                mx = jnp.maximum(lp, lse)
                tot = mx + jnp.log2(jnp.exp2(lp - mx) + jnp.exp2(lse - mx))
                o = so_ref[dst, :] * jnp.exp2(lp - tot) + o * jnp.exp2(lse - tot)
                lse = tot
            so_ref[dst, :] = o
            if bi < len(dilations) - 1:
                sl_ref[dst, :] = lse
            return carry

        lax.fori_loop(0, tb // DIL_TILE, tile, 0, unroll=DIL_UNROLL)

    o_ref[...] = so_ref[...].astype(o_ref.dtype)


def dilated_mixture(qkv):
    bsz, seq, three_w = qkv.shape
    width = three_w // 3
    n_pairs = width // LANES
    dilations = tuple(sorted((d for _, d in DILATION_PAIRS), reverse=True))
    halves = {w // (2 * d) for w, d in DILATION_PAIRS}
    assert len(halves) == 1
    half = halves.pop()
    halo = half * max(dilations)
    tb = DIL_TILE * max(dilations)
    assert seq % tb == 0 and tb % halo == 0 and all(max(dilations) % d == 0 for d in dilations)
    per = tb // halo
    n_halo = seq // halo

    def centre(col):
        return pl.BlockSpec((None, tb, LANES), lambda b, i, p: (b, i, col * n_pairs + p))

    def before(col):
        return pl.BlockSpec((None, halo, LANES),
                            lambda b, i, p: (b, jnp.maximum(i * per - 1, 0), col * n_pairs + p))

    def after(col):
        return pl.BlockSpec((None, halo, LANES),
                            lambda b, i, p: (b, jnp.minimum((i + 1) * per, n_halo - 1),
                                             col * n_pairs + p))

    return pl.pallas_call(
        functools.partial(_dilated_kernel, seq=seq, half=half, dilations=dilations),
        grid=(bsz, seq // tb, n_pairs),
        in_specs=[centre(0), before(1), centre(1), after(1), before(2), centre(2), after(2)],
        out_specs=pl.BlockSpec((None, tb, LANES), lambda b, i, p: (b, i, p)),
        out_shape=jax.ShapeDtypeStruct((bsz, seq, width), BF16),
        scratch_shapes=[pltpu.VMEM((tb, LANES), F32),
                        pltpu.VMEM((tb + 2 * halo, LANES), F32),
                        pltpu.VMEM((tb + 2 * halo, LANES), F32),
                        pltpu.VMEM((tb, LANES), F32),
                        pltpu.VMEM((tb, LANES), F32)],
        compiler_params=_cparams(("parallel", "parallel", "parallel")),
        name="dilated_mixture",
    )(*([qkv] * 7))


def _norm_matmul_t_kernel(x_ref, g_ref, w_ref, o_ref):
    h = _rms(x_ref[...], g_ref[...]).astype(BF16)
    acc_t = jnp.dot(h, w_ref[...], preferred_element_type=F32).T
    n_chunk, _, tk = o_ref.shape
    for c in range(n_chunk):
        o_ref[c] = acc_t[:, c * tk:(c + 1) * tk].astype(o_ref.dtype)


def norm_matmul_t(x, g, w, col0, n, bsz, seq, tk):
    t, d = x.shape
    tm = _pick(seq, (512, 256, 128))
    tn = _pick(n, (1024, 512, 256, 128))
    j0 = _col_block(col0, tn)
    assert tm % tk == 0
    per = tm // tk
    nblk_s = seq // tm
    return pl.pallas_call(
        _norm_matmul_t_kernel,
        grid=(t // tm, n // tn),
        in_specs=[pl.BlockSpec((tm, d), lambda i, j: (i, 0)),
                  pl.BlockSpec((1, d), lambda i, j: (0, 0)),
                  pl.BlockSpec((d, tn), lambda i, j: (0, j0 + j))],
        out_specs=pl.BlockSpec((None, per, tn, tk), lambda i, j: (i // nblk_s, i % nblk_s, j, 0)),
        out_shape=jax.ShapeDtypeStruct((bsz, seq // tk, n, tk), BF16),
        compiler_params=_cparams(("parallel", "parallel")),
        name="norm_matmul_t",
    )(x, g, w)


def _diff_attn_kernel(q_ref, k_ref, vt_ref, lq1_ref, lk1_ref, lq2_ref, lk2_ref, sub_ref, o_ref,
                      sa_ref, sb_ref, acc_ref, *, lam_init):
    tq = q_ref.shape[0]
    n_kt, _, tk = vt_ref.shape
    q = q_ref[...]
    lane = lax.broadcasted_iota(jnp.int32, (tq, LANES), 1)
    lo = lane < HEAD_DIM
    q1 = jnp.where(lo, q, jnp.zeros_like(q))
    q2 = jnp.where(lo, jnp.zeros_like(q), q)

    def scores(kt, s_ref):
        start = kt * tk if isinstance(kt, int) else pl.multiple_of(kt * tk, tk)
        k = k_ref[pl.ds(start, tk), :]
        cms = []
        for j, qm in enumerate((q1, q2)):
            st = lax.dot_general(k, qm, _NT, preferred_element_type=F32)
            s_ref[j] = st
            cms.append(jnp.max(st, axis=0, keepdims=True))
        return tuple(cms)

    ones = jnp.ones((ONES_ROWS, tk), BF16)

    def absorb(kt, s_ref, ms, cms):
        vt = jnp.concatenate([vt_ref[kt], ones], axis=0)
        out = []
        for j in range(2):
            m_new = jnp.maximum(ms[j], cms[j])
            alpha = jnp.exp2(ms[j] - m_new)
            p = jnp.exp2((s_ref[j] - m_new).astype(BF16))
            acc_ref[j] = alpha * acc_ref[j] + jnp.dot(vt, p, preferred_element_type=F32)
            out.append(m_new)
        return tuple(out)

    bufs = (sa_ref, sb_ref)
    unroll = min(DIFF_UNROLL, n_kt)
    m0 = jnp.full((1, tq), NEG, F32)
    acc_ref[...] = jnp.zeros_like(acc_ref)
    ms, cms = (m0, m0), scores(0, sa_ref)
    if n_kt == unroll:
        for kt in range(n_kt):
            if kt + 1 < n_kt:
                cms_next = scores(kt + 1, bufs[(kt + 1) % 2])
            ms = absorb(kt, bufs[kt % 2], ms, cms)
            cms = cms_next
    else:
        def body(it, carry):
            ms, cms = carry[:2], carry[2:]
            kt = unroll * it
            for u in range(unroll):
                cms_next = scores(jnp.minimum(kt + u + 1, n_kt - 1), bufs[(u + 1) % 2])
                ms = absorb(kt + u, bufs[u % 2], ms, cms)
                cms = cms_next
            return ms + cms

        lax.fori_loop(0, n_kt // unroll, body, ms + cms)
    a1, l1 = acc_ref[0, :LANES], acc_ref[0, LANES:LANES + 1]
    a2, l2 = acc_ref[1, :LANES], acc_ref[1, LANES:LANES + 1]

    lam = (jnp.exp(jnp.sum(lq1_ref[...] * lk1_ref[...], axis=1, keepdims=True))
           - jnp.exp(jnp.sum(lq2_ref[...] * lk2_ref[...], axis=1, keepdims=True)) + lam_init)
    o_t = a1 * (1.0 / l1) - lam * (a2 * (1.0 / l2))
    o_t = o_t * (lax.rsqrt(jnp.mean(o_t * o_t, axis=0, keepdims=True) + EPS) * (1.0 - lam_init))
    o_ref[...] = (o_t.T * sub_ref[...]).astype(o_ref.dtype)


def diff_attention(qk, vt, lq1, lk1, lq2, lk2, subln, lam_init):
    bsz, seq, two_d = qk.shape
    d = two_d // 2
    n_heads = d // LANES
    n_kt, tk = vt.shape[1], vt.shape[3]
    assert DIFF_UNROLL % 2 == 0 and n_kt % min(DIFF_UNROLL, n_kt) == 0
    tq = _pick(seq, (512, 256, 128))
    row = lambda a: a.reshape(1, -1).astype(F32)
    small = lambda n: pl.BlockSpec((1, n), lambda b, h, i: (0, 0))
    return pl.pallas_call(
        functools.partial(_diff_attn_kernel, lam_init=lam_init),
        grid=(bsz, n_heads, seq // tq),
        in_specs=[pl.BlockSpec((None, tq, LANES), lambda b, h, i: (b, i, h)),
                  pl.BlockSpec((None, seq, LANES), lambda b, h, i: (b, 0, n_heads + h)),
                  pl.BlockSpec((None, n_kt, LANES, tk), lambda b, h, i: (b, 0, h, 0)),
                  small(HEAD_DIM), small(HEAD_DIM), small(HEAD_DIM), small(HEAD_DIM),
                  small(LANES)],
        out_specs=pl.BlockSpec((None, tq, LANES), lambda b, h, i: (b, i, h)),
        out_shape=jax.ShapeDtypeStruct((bsz, seq, d), BF16),
        scratch_shapes=[pltpu.VMEM((2, tk, tq), F32), pltpu.VMEM((2, tk, tq), F32),
                        pltpu.VMEM((2, LANES + ONES_ROWS, tq), F32)],
        compiler_params=_cparams(("parallel", "parallel", "parallel")),
        name="diff_attention",
    )(qk, qk, vt, row(lq1), row(lk1), row(lq2), row(lk2), row(subln))


def _out_proj_kernel(*refs, n_in):
    ys = refs[:n_in]
    ws = refs[n_in:2 * n_in]
    x_ref, g_ref, o_ref = refs[2 * n_in:]
    m = jnp.dot(ys[0][...], ws[0][...], preferred_element_type=F32)
    for y_ref, w_ref in zip(ys[1:], ws[1:]):
        m = m + jnp.dot(y_ref[...], w_ref[...], preferred_element_type=F32)
    o_ref[...] = x_ref[...] + _rms(m, g_ref[...])


def out_proj_residual(ys, w, x, g):
    t, d = x.shape
    tm = _pick(t, (512, 256, 128))
    kdim = ys[0].shape[1]
    assert all(y.shape[1] == kdim for y in ys) and w.shape[0] == kdim * len(ys)
    in_specs = ([pl.BlockSpec((tm, kdim), lambda i: (i, 0)) for _ in ys]
                + [pl.BlockSpec((kdim, d), lambda i, r=r: (r, 0)) for r in range(len(ys))]
                + [pl.BlockSpec((tm, d), lambda i: (i, 0)), pl.BlockSpec((1, d), lambda i: (0, 0))])
    return pl.pallas_call(
        functools.partial(_out_proj_kernel, n_in=len(ys)),
        grid=(t // tm,),
        in_specs=in_specs,
        out_specs=pl.BlockSpec((tm, d), lambda i: (i, 0)),
        out_shape=jax.ShapeDtypeStruct((t, d), F32),
        compiler_params=_cparams(("parallel",)),
        name="out_proj",
    )(*ys, *([w] * len(ys)), x, g)


def _ffn_kernel(x_ref, gpre_ref, gpost_ref, wg_ref, wu_ref, wd_ref, o_ref, h_ref, acc_ref):
    f = pl.program_id(1)

    @pl.when(f == 0)
    def _():
        h_ref[...] = _rms(x_ref[...], gpre_ref[...]).astype(BF16)
        acc_ref[...] = jnp.zeros_like(acc_ref)

    h = h_ref[...]
    gate = jnp.dot(h, wg_ref[...], preferred_element_type=F32)
    up = jnp.dot(h, wu_ref[...], preferred_element_type=F32)
    act = (gate / (1.0 + jnp.exp(-gate))) * up
    acc_ref[...] += jnp.dot(act.astype(BF16), wd_ref[...], preferred_element_type=F32)

    @pl.when(f == pl.num_programs(1) - 1)
    def _():
        o_ref[...] = x_ref[...] + _rms(acc_ref[...], gpost_ref[...])


def ffn_residual(x, g_pre, g_post, w_gate, w_up, w_down):
    t, d = x.shape
    f = w_gate.shape[1]
    tm = _pick(t, (512, 256, 128))
    tf = _pick(f, (512, 256, 128))
    return pl.pallas_call(
        _ffn_kernel,
        grid=(t // tm, f // tf),
        in_specs=[pl.BlockSpec((tm, d), lambda i, j: (i, 0)),
                  pl.BlockSpec((1, d), lambda i, j: (0, 0)),
                  pl.BlockSpec((1, d), lambda i, j: (0, 0)),
                  pl.BlockSpec((d, tf), lambda i, j: (0, j)),
                  pl.BlockSpec((d, tf), lambda i, j: (0, j)),
                  pl.BlockSpec((tf, d), lambda i, j: (j, 0))],
        out_specs=pl.BlockSpec((tm, d), lambda i, j: (i, 0)),
        out_shape=jax.ShapeDtypeStruct((t, d), F32),
        scratch_shapes=[pltpu.VMEM((tm, d), BF16), pltpu.VMEM((tm, d), F32)],
        compiler_params=_cparams(("parallel", "arbitrary")),
        name="ffn",
    )(x, g_pre, g_post, w_gate, w_up, w_down)


def kernel(x, norm_mix_pre, norm_mix_post, norm_ffn_pre, norm_ffn_post, ev_w_in, ev_conv_w, ev_conv_b, ev_w_r, ev_b_r, ev_w_i, ev_b_i, ev_lam, ev_w_out, od_w_in, od_lam_q1, od_lam_k1, od_lam_q2, od_lam_k2, od_subln, od_w_out, ffn_w_gate, ffn_w_up, ffn_w_down):
    bsz, seq, d = x.shape
    depth = norm_mix_pre.shape[0]
    t = bsz * seq
    aw = ev_conv_w.shape[-1]
    rope = rope_tables(seq)
    gain = lambda g: g.reshape(1, d).astype(F32)
    xf = x.reshape(t, d)
    for layer in range(depth):
        j = layer // 2
        g_pre = gain(norm_mix_pre[layer])
        if layer % 2 == 0:
            w_in = ev_w_in[j].astype(BF16)
            z_a = norm_matmul(xf, g_pre, w_in, 0, 2 * aw, F32).reshape(bsz, seq, 2 * aw)
            qkv = qkv_proj(xf, g_pre, w_in, 2 * aw, w_in.shape[1] - 2 * aw, rope, seq, 3)
            rows = lambda p: p.reshape(1, aw).astype(F32)
            h_fwd = None
            for direction in (0, 1):
                h_fwd = rglru_direction(
                    z_a, ev_conv_w[j].astype(F32), rows(ev_conv_b[j]),
                    gate_group_weights(ev_w_r[j, direction], ev_w_i[j, direction]),
                    rows(ev_b_r[j, direction]), rows(ev_b_i[j, direction]),
                    rows(ev_lam[j, direction]), h_fwd, reverse=direction == 1)
            y_a = h_fwd.reshape(t, aw)
            y_b = dilated_mixture(qkv.reshape(bsz, seq, -1)).reshape(t, -1)
            ys, w_out = [y_a, y_b], ev_w_out[j].astype(BF16)
        else:
            lam_init = 0.8 - 0.6 * math.exp(-0.3 * layer)
            w_in = od_w_in[j].astype(BF16)
            qk = qkv_proj(xf, g_pre, w_in, 0, 2 * d, rope, seq, 2)
            vt = norm_matmul_t(xf, g_pre, w_in, 2 * d, d, bsz, seq, _pick(seq, (DIFF_TK, LANES)))
            y = diff_attention(qk.reshape(bsz, seq, -1), vt, od_lam_q1[j], od_lam_k1[j],
                               od_lam_q2[j], od_lam_k2[j], od_subln[j], lam_init)
            ys, w_out = [y.reshape(t, -1)], od_w_out[j].astype(BF16)
        xf = out_proj_residual(ys, w_out, xf, gain(norm_mix_post[layer]))
        xf = ffn_residual(xf, gain(norm_ffn_pre[layer]), gain(norm_ffn_post[layer]),
                          ffn_w_gate[layer].astype(BF16), ffn_w_up[layer].astype(BF16),
                          ffn_w_down[layer].astype(BF16))
    return xf.reshape(bsz, seq, d)
```

```python
import functools
import math

import jax
import jax.numpy as jnp
from jax import lax
from jax.experimental import pallas as pl
from jax.experimental.pallas import tpu as pltpu

F32 = jnp.float32
BF16 = jnp.bfloat16

EPS = 1e-6
NEG = -1e30
HEAD_DIM = 64
ROT_DIM = HEAD_DIM // 4
ROT_HALF = ROT_DIM // 2
ROPE_THETA = 500000.0
CONV_WIDTH = 4
RG_C = 8.0
DILATION_PAIRS = ((128, 1), (512, 4), (2048, 16))
GELU_C = math.sqrt(2.0 / math.pi)

LANES = 128
SUBLANES = 8
MXU_DIM = 256
VMEM_LIMIT_BYTES = 56 * 1024 * 1024

DIFF_TK = 256
DIFF_QK_SPAN = 4
BF16_SUBLANES = 16
ONES_ROWS = BF16_SUBLANES

LOG2E = math.log2(math.e)
Q_SCALE = HEAD_DIM ** -0.5 * LOG2E

_NT = (((1,), (1,)), ((), ()))


def _pick(n, prefs):
    for p in prefs:
        if n % p == 0:
            return p
    return n


def _cparams(sem):
    return pltpu.CompilerParams(dimension_semantics=sem, vmem_limit_bytes=VMEM_LIMIT_BYTES)


def _rms(x, g):
    return x * lax.rsqrt(jnp.mean(x * x, axis=-1, keepdims=True) + EPS) * g


def _norm_matmul_kernel(x_ref, g_ref, w_ref, o_ref):
    h = _rms(x_ref[...], g_ref[...]).astype(BF16)
    o_ref[...] = jnp.dot(h, w_ref[...], preferred_element_type=F32).astype(o_ref.dtype)


def _col_block(col0, tn):
    assert col0 % tn == 0
    return col0 // tn


def norm_matmul(x, g, w, col0, n, out_dtype):
    t, d = x.shape
    tm = _pick(t, (512, 256, 128))
    tn = _pick(n, (2048, 1024, 512, 256, 128))
    j0 = _col_block(col0, tn)
    return pl.pallas_call(
        _norm_matmul_kernel,
        grid=(t // tm, n // tn),
        in_specs=[pl.BlockSpec((tm, d), lambda i, j: (i, 0)),
                  pl.BlockSpec((1, d), lambda i, j: (0, 0)),
                  pl.BlockSpec((d, tn), lambda i, j: (0, j0 + j))],
        out_specs=pl.BlockSpec((tm, tn), lambda i, j: (i, j)),
        out_shape=jax.ShapeDtypeStruct((t, n), out_dtype),
        compiler_params=_cparams(("parallel", "parallel")),
        name="norm_matmul",
    )(x, g, w)


def _qkv_kernel(x_ref, g_ref, w_ref, c_ref, sa_ref, sb_ref, o_ref):
    h = _rms(x_ref[...], g_ref[...]).astype(BF16)
    acc = jnp.dot(h, w_ref[...], preferred_element_type=F32)
    c, sa, sb = c_ref[...], sa_ref[...], sb_ref[...]
    for cb in range(acc.shape[1] // LANES):
        a = acc[:, cb * LANES:(cb + 1) * LANES]
        r = a * c + pltpu.roll(a, LANES - ROT_HALF, 1) * sa + pltpu.roll(a, ROT_HALF, 1) * sb
        o_ref[:, cb * LANES:(cb + 1) * LANES] = r.astype(o_ref.dtype)


def qkv_proj(x, g, w, col0, n, rope, seq, n_parts):
    t, d = x.shape
    width = n // n_parts
    tm = _pick(seq, (512, 256, 128))
    tn = _pick(width, (2048, 1024, 512, 256, 128))
    per_part = width // tn
    j0 = _col_block(col0, tn)
    nblk_s = seq // tm
    tab_spec = pl.BlockSpec((None, tm, LANES), lambda i, j: (j // per_part, i % nblk_s, 0))
    return pl.pallas_call(
        _qkv_kernel,
        grid=(t // tm, n // tn),
        in_specs=[pl.BlockSpec((tm, d), lambda i, j: (i, 0)),
                  pl.BlockSpec((1, d), lambda i, j: (0, 0)),
                  pl.BlockSpec((d, tn), lambda i, j: (0, j0 + j)),
                  tab_spec, tab_spec, tab_spec],
        out_specs=pl.BlockSpec((tm, tn), lambda i, j: (i, j)),
        out_shape=jax.ShapeDtypeStruct((t, n), BF16),
        compiler_params=_cparams(("parallel", "parallel")),
        name="qkv_proj",
    )(x, g, w, *rope)


def rope_tables(seq):
    pos = jnp.arange(seq, dtype=F32)
    inv = ROPE_THETA ** (-jnp.arange(0, ROT_DIM, 2, dtype=F32) / ROT_DIM)
    ang = pos[:, None] * inv[None, :]
    cos, sin = jnp.cos(ang), jnp.sin(ang)
    zeros = jnp.zeros((seq, HEAD_DIM - ROT_DIM), F32)
    zh = jnp.zeros((seq, ROT_HALF), F32)
    c = jnp.concatenate([cos, cos, zeros + 1.0], axis=1)
    sa = jnp.concatenate([-sin, zh, zeros], axis=1)
    sb = jnp.concatenate([zh, sin, zeros], axis=1)
    rep = LANES // HEAD_DIM
    ident = (jnp.ones_like(c), jnp.zeros_like(c), jnp.zeros_like(c))
    return tuple(jnp.tile(jnp.stack([tab * Q_SCALE, tab, idt]), (1, 1, rep))
                 for tab, idt in zip((c, sa, sb), ident))


def _rglru_kernel(*refs, reverse, final, n_blk):
    if final:
        (xp_ref, xc_ref, xn_ref, cw_ref, cb_ref, wg_ref, br_ref, bi_ref, lam_ref,
         gate_ref, prev_ref, o_ref, carry_ref) = refs
    else:
        (xp_ref, xc_ref, xn_ref, cw_ref, cb_ref, wg_ref, br_ref, bi_ref, lam_ref,
         o_ref, carry_ref) = refs
    i = pl.program_id(1)
    blk = n_blk - 1 - i if reverse else i
    tb, aw = xc_ref.shape

    @pl.when(i == 0)
    def _():
        carry_ref[...] = jnp.zeros_like(carry_ref)

    x0 = xc_ref[...]
    xp = jnp.where(blk > 0, xp_ref[...], 0.0)
    xn = jnp.where(blk < n_blk - 1, xn_ref[...], 0.0)
    row = lax.broadcasted_iota(jnp.int32, (tb, aw), 0)
    zpad = jnp.zeros((tb - SUBLANES, aw), F32)
    conv = cb_ref[...] + cw_ref[2:3, :] * x0
    for back in (1, 2):
        head = jnp.concatenate([pltpu.roll(xp, back, 0), zpad], axis=0)
        shifted = jnp.where(row < back, head, pltpu.roll(x0, back, 0))
        conv = conv + cw_ref[2 - back:3 - back, :] * shifted
    tail = jnp.concatenate([zpad, pltpu.roll(xn, SUBLANES - 1, 0)], axis=0)
    shifted = jnp.where(row == tb - 1, tail, pltpu.roll(x0, tb - 1, 0))
    conv = conv + cw_ref[3:4, :] * shifted

    n_grp, gw, _ = wg_ref.shape
    cbf = conv.astype(BF16)
    zr, zi = [], []
    for g in range(n_grp):
        z = jnp.dot(cbf[:, g * gw:(g + 1) * gw], wg_ref[g], preferred_element_type=F32)
        zr.append(z[:, :gw])
        zi.append(z[:, gw:])
    zr = jnp.concatenate(zr, axis=1) if n_grp > 1 else zr[0]
    zi = jnp.concatenate(zi, axis=1) if n_grp > 1 else zi[0]
    r = 1.0 / (1.0 + jnp.exp(-(zr + br_ref[...])))
    ig = 1.0 / (1.0 + jnp.exp(-(zi + bi_ref[...])))
    nl = -lam_ref[...]
    softplus = jnp.maximum(nl, 0.0) + jnp.log(1.0 + jnp.exp(-jnp.abs(nl)))
    a = jnp.exp(-RG_C * r * softplus)
    u = jnp.sqrt(jnp.maximum(1.0 - a * a, 0.0)) * (ig * conv)

    n_grp8 = tb // SUBLANES
    a3 = a.reshape(n_grp8, SUBLANES, aw)
    u3 = u.reshape(n_grp8, SUBLANES, aw)
    sub = lax.broadcasted_iota(jnp.int32, a3.shape, 1)
    s = 1
    while s < SUBLANES:
        if reverse:
            keep = sub < SUBLANES - s
            a_sh = jnp.where(keep, pltpu.roll(a3, SUBLANES - s, 1), 1.0)
            u_sh = jnp.where(keep, pltpu.roll(u3, SUBLANES - s, 1), 0.0)
        else:
            keep = sub >= s
            a_sh = jnp.where(keep, pltpu.roll(a3, s, 1), 1.0)
            u_sh = jnp.where(keep, pltpu.roll(u3, s, 1), 0.0)
        u3 = a3 * u_sh + u3
        a3 = a3 * a_sh
        s *= 2
    edge = 0 if reverse else SUBLANES - 1
    state = carry_ref[edge:edge + 1, :]
    groups = [None] * n_grp8
    for j in (range(n_grp8 - 1, -1, -1) if reverse else range(n_grp8)):
        groups[j] = a3[j] * state + u3[j]
        state = groups[j][edge:edge + 1, :]
    h = jnp.concatenate(groups, axis=0)
    carry_ref[...] = groups[0] if reverse else groups[-1]

    if final:
        gt = gate_ref[...]
        gelu = 0.5 * gt * (1.0 + jnp.tanh(GELU_C * (gt + 0.044715 * (gt * gt * gt))))
        o_ref[...] = ((prev_ref[...] + h) * gelu).astype(o_ref.dtype)
    else:
        o_ref[...] = h


def rglru_direction(z_a, conv_w, conv_b, w_gate, b_r, b_i, lam, prev, reverse):
    bsz, seq, two_aw = z_a.shape
    aw = two_aw // 2
    tb = _pick(seq, (256, 128, 64, 32, 16, 8))
    n_blk = seq // tb
    rows8 = seq // SUBLANES
    per8 = tb // SUBLANES
    final = prev is not None

    def blk(i):
        return n_blk - 1 - i if reverse else i

    xr_specs = [
        pl.BlockSpec((None, SUBLANES, aw), lambda b, i: (b, jnp.maximum(blk(i) * per8 - 1, 0), 1)),
        pl.BlockSpec((None, tb, aw), lambda b, i: (b, blk(i), 1)),
        pl.BlockSpec((None, SUBLANES, aw),
                     lambda b, i: (b, jnp.minimum((blk(i) + 1) * per8, rows8 - 1), 1)),
    ]
    full = lambda a: pl.BlockSpec(a.shape, lambda b, i: (0,) * a.ndim)
    params = [conv_w, conv_b, w_gate, b_r, b_i, lam]
    in_specs = xr_specs + [full(p) for p in params]
    args = [z_a, z_a, z_a] + params
    if final:
        in_specs += [pl.BlockSpec((None, tb, aw), lambda b, i: (b, blk(i), 0)),
                     pl.BlockSpec((None, tb, aw), lambda b, i: (b, blk(i), 0))]
        args += [z_a, prev]
    return pl.pallas_call(
        functools.partial(_rglru_kernel, reverse=reverse, final=final, n_blk=n_blk),
        grid=(bsz, n_blk),
        in_specs=in_specs,
        out_specs=pl.BlockSpec((None, tb, aw), lambda b, i: (b, blk(i), 0)),
        out_shape=jax.ShapeDtypeStruct((bsz, seq, aw), BF16 if final else F32),
        scratch_shapes=[pltpu.VMEM((SUBLANES, aw), F32)],
        compiler_params=_cparams(("parallel", "arbitrary")),
        name="rglru_bwd" if reverse else "rglru_fwd",
    )(*args)


def gate_group_weights(w_r, w_i):
    n, bs, _ = w_r.shape
    aw = n * bs
    gw = _pick(aw, (MXU_DIM, LANES))
    per = gw // bs
    n_grp = aw // gw
    eye = jnp.eye(per, dtype=w_r.dtype)

    def dense(w):
        wg = w.reshape(n_grp, per, bs, bs)
        return jnp.einsum('gaef,ab->gaebf', wg, eye).reshape(n_grp, gw, gw)

    return jnp.concatenate([dense(w_r), dense(w_i)], axis=2).astype(BF16)


DIL_TILE = 128
DIL_UNROLL = 16


def _dilated_kernel(q_ref, kp_ref, kc_ref, kn_ref, vp_ref, vc_ref, vn_ref, o_ref,
                    qbuf, kbuf, vbuf, so_ref, sl_ref, *, seq, half, dilations):
    tb = q_ref.shape[0]
    halo = kp_ref.shape[0]
    t0 = pl.program_id(1) * tb
    qbuf[...] = q_ref[...].astype(F32)
    for buf, prev, cur, nxt in ((kbuf, kp_ref, kc_ref, kn_ref), (vbuf, vp_ref, vc_ref, vn_ref)):
        buf[0:halo] = prev[...].astype(F32)
        buf[halo:halo + tb] = cur[...].astype(F32)
        buf[halo + tb:halo + tb + halo] = nxt[...].astype(F32)

    nk = DIL_TILE + 2 * half
    rowq = lax.broadcasted_iota(jnp.int32, (DIL_TILE, nk), 0)
    colk = lax.broadcasted_iota(jnp.int32, (DIL_TILE, nk), 1)
    band_bias = jnp.where(jnp.abs(colk - half - rowq) <= half, 0.0, NEG).astype(F32)
    col1 = lax.broadcasted_iota(jnp.int32, (1, nk), 1)
    lane = lax.broadcasted_iota(jnp.int32, (DIL_TILE, LANES), 1)
    lo = lane < HEAD_DIM
    lo_k = lax.broadcasted_iota(jnp.int32, (nk, LANES), 1) < HEAD_DIM

    for bi, d in enumerate(dilations):
        span = DIL_TILE * d

        def rows(ref, start, size, d=d):
            if d == 1:
                return ref[pl.ds(pl.multiple_of(start, SUBLANES), size), :]
            return ref[pl.ds(start, size, stride=d), :]

        def tile(n, carry, bi=bi, d=d, span=span, rows=rows):
            g, r = n // d, n % d
            qs = g * span + r
            ks = halo + qs - half * d
            tok = (t0 - halo + ks) + col1 * d
            bias = band_bias + jnp.where((tok >= 0) & (tok < seq), 0.0, NEG)
            q = rows(qbuf, qs, DIL_TILE).astype(BF16)
            k = rows(kbuf, ks, nk).astype(BF16)
            v = rows(vbuf, ks, nk).astype(BF16)
            res, ms = None, []
            for first in (True, False):
                own = lo if first else ~lo
                qm = jnp.where(own, q, jnp.zeros_like(q))
                s = lax.dot_general(qm, k, _NT, preferred_element_type=F32) + bias
                m = jnp.max(s, axis=1, keepdims=True)
                p = jnp.exp2((s - m).astype(BF16))
                own_k = lo_k if first else ~lo_k
                v_ext = jnp.concatenate(
                    [jnp.where(own_k, v, jnp.zeros_like(v)),
                     jnp.where(own_k, 1.0, 0.0).astype(BF16)], axis=1)
                pv = jnp.dot(p, v_ext, preferred_element_type=F32)
                res = pv if res is None else res + pv
                ms.append(m)
            den = res[:, LANES:]
            o = res[:, :LANES] / den
            lse = jnp.where(lo, ms[0], ms[1]) + jnp.log2(den)
            if d == 1:
                dst = pl.ds(pl.multiple_of(qs, SUBLANES), DIL_TILE)
            else:
                dst = pl.ds(qs, DIL_TILE, stride=d)
            if bi > 0:
                lp = sl_ref[dst, :]
                mx = jnp.maximum(lp, lse)
                tot = mx + jnp.log2(jnp.exp2(lp - mx) + jnp.exp2(lse - mx))
                o = so_ref[dst, :] * jnp.exp2(lp - tot) + o * jnp.exp2(lse - tot)
                lse = tot
            so_ref[dst, :] = o
            if bi < len(dilations) - 1:
                sl_ref[dst, :] = lse
            return carry

        lax.fori_loop(0, tb // DIL_TILE, tile, 0, unroll=DIL_UNROLL)

    o_ref[...] = so_ref[...].astype(o_ref.dtype)


def dilated_mixture(qkv):
    bsz, seq, three_w = qkv.shape
    width = three_w // 3
    n_pairs = width // LANES
    dilations = tuple(sorted((d for _, d in DILATION_PAIRS), reverse=True))
    halves = {w // (2 * d) for w, d in DILATION_PAIRS}
    assert len(halves) == 1
    half = halves.pop()
    halo = half * max(dilations)
    tb = DIL_TILE * max(dilations)
    assert seq % tb == 0 and tb % halo == 0 and all(max(dilations) % d == 0 for d in dilations)
    per = tb // halo
    n_halo = seq // halo

    def centre(col):
        return pl.BlockSpec((None, tb, LANES), lambda b, i, p: (b, i, col * n_pairs + p))

    def before(col):
        return pl.BlockSpec((None, halo, LANES),
                            lambda b, i, p: (b, jnp.maximum(i * per - 1, 0), col * n_pairs + p))

    def after(col):
        return pl.BlockSpec((None, halo, LANES),
                            lambda b, i, p: (b, jnp.minimum((i + 1) * per, n_halo - 1),
                                             col * n_pairs + p))

    return pl.pallas_call(
        functools.partial(_dilated_kernel, seq=seq, half=half, dilations=dilations),
        grid=(bsz, seq // tb, n_pairs),
        in_specs=[centre(0), before(1), centre(1), after(1), before(2), centre(2), after(2)],
        out_specs=pl.BlockSpec((None, tb, LANES), lambda b, i, p: (b, i, p)),
        out_shape=jax.ShapeDtypeStruct((bsz, seq, width), BF16),
        scratch_shapes=[pltpu.VMEM((tb, LANES), F32),
                        pltpu.VMEM((tb + 2 * halo, LANES), F32),
                        pltpu.VMEM((tb + 2 * halo, LANES), F32),
                        pltpu.VMEM((tb, LANES), F32),
                        pltpu.VMEM((tb, LANES), F32)],
        compiler_params=_cparams(("parallel", "parallel", "parallel")),
        name="dilated_mixture",
    )(*([qkv] * 7))


def _norm_matmul_t_kernel(x_ref, g_ref, w_ref, o_ref):
    h = _rms(x_ref[...], g_ref[...]).astype(BF16)
    acc_t = jnp.dot(h, w_ref[...], preferred_element_type=F32).T
    n_chunk, _, tk = o_ref.shape
    for c in range(n_chunk):
        o_ref[c] = acc_t[:, c * tk:(c + 1) * tk].astype(o_ref.dtype)


def norm_matmul_t(x, g, w, col0, n, bsz, seq, tk):
    t, d = x.shape
    tm = _pick(seq, (512, 256, 128))
    tn = _pick(n, (1024, 512, 256, 128))
    j0 = _col_block(col0, tn)
    assert tm % tk == 0
    per = tm // tk
    nblk_s = seq // tm
    return pl.pallas_call(
        _norm_matmul_t_kernel,
        grid=(t // tm, n // tn),
        in_specs=[pl.BlockSpec((tm, d), lambda i, j: (i, 0)),
                  pl.BlockSpec((1, d), lambda i, j: (0, 0)),
                  pl.BlockSpec((d, tn), lambda i, j: (0, j0 + j))],
        out_specs=pl.BlockSpec((None, per, tn, tk), lambda i, j: (i // nblk_s, i % nblk_s, j, 0)),
        out_shape=jax.ShapeDtypeStruct((bsz, seq // tk, n, tk), BF16),
        compiler_params=_cparams(("parallel", "parallel")),
        name="norm_matmul_t",
    )(x, g, w)


def _diff_attn_kernel(q_ref, k_ref, vt_ref, lq1_ref, lk1_ref, lq2_ref, lk2_ref, sub_ref, o_ref,
                      sa_ref, sb_ref, acc_ref, *, lam_init, span):
    tq = q_ref.shape[0]
    n_kt, _, tk = vt_ref.shape
    q = q_ref[...]
    lane = lax.broadcasted_iota(jnp.int32, (tq, LANES), 1)
    lo = lane < HEAD_DIM
    q1 = jnp.where(lo, q, jnp.zeros_like(q))
    q2 = jnp.where(lo, jnp.zeros_like(q), q)

    n_qt = tq // MXU_DIM if tq % MXU_DIM == 0 else 1
    qt = tq // n_qt
    q_tiles = [(q1[n * qt:(n + 1) * qt], q2[n * qt:(n + 1) * qt]) for n in range(n_qt)]
    pieces = [(n, j) for n in range(n_qt) for j in range(2)]
    ones = jnp.ones((ONES_ROWS, tk), BF16)
    n_sp = n_kt // span

    def score_piece(sp, s_ref, n, j):
        k = k_ref[pl.ds(sp * span * tk, span * tk), :]
        st = lax.dot_general(k, q_tiles[n][j], _NT, preferred_element_type=F32)
        s_ref[j, :, n * qt:(n + 1) * qt] = st
        return [jnp.max(st[h * tk:(h + 1) * tk], axis=0, keepdims=True) for h in range(span)]

    def softmax_piece(s_ref, h, n, j, m, cm):
        m_new = jnp.maximum(m, cm)
        alpha = jnp.exp2(m - m_new)
        st = s_ref[j, h * tk:(h + 1) * tk, n * qt:(n + 1) * qt]
        return m_new, alpha, jnp.exp2((st - m_new).astype(BF16))

    def pv_piece(vt, n, j, alpha, p):
        cols = slice(n * qt, (n + 1) * qt)
        acc_ref[j, :, cols] = alpha * acc_ref[j, :, cols] + jnp.dot(vt, p, preferred_element_type=F32)

    bufs = (sa_ref, sb_ref)
    acc_ref[...] = jnp.zeros_like(acc_ref)
    ms = {p: jnp.full((1, qt), NEG, F32) for p in pieces}
    cms = {p: score_piece(0, sa_ref, *p) for p in pieces}
    for sp in range(n_sp):
        vts = [jnp.concatenate([vt_ref[sp * span + h], ones], axis=0) for h in range(span)]
        cms_next = {}
        for n, j in pieces:
            if sp + 1 < n_sp:
                cms_next[n, j] = score_piece(sp + 1, bufs[(sp + 1) % 2], n, j)
            for h in range(span):
                ms[n, j], alpha, p = softmax_piece(bufs[sp % 2], h, n, j, ms[n, j], cms[n, j][h])
                pv_piece(vts[h], n, j, alpha, p)
        cms = cms_next
    a1, l1 = acc_ref[0, :LANES], acc_ref[0, LANES:LANES + 1]
    a2, l2 = acc_ref[1, :LANES], acc_ref[1, LANES:LANES + 1]

    lam = (jnp.exp(jnp.sum(lq1_ref[...] * lk1_ref[...], axis=1, keepdims=True))
           - jnp.exp(jnp.sum(lq2_ref[...] * lk2_ref[...], axis=1, keepdims=True)) + lam_init)
    o_t = a1 * (1.0 / l1) - lam * (a2 * (1.0 / l2))
    o_t = o_t * (lax.rsqrt(jnp.mean(o_t * o_t, axis=0, keepdims=True) + EPS) * (1.0 - lam_init))
    o_ref[...] = (o_t.T * sub_ref[...]).astype(o_ref.dtype)


def diff_attention(qk, vt, lq1, lk1, lq2, lk2, subln, lam_init):
    bsz, seq, two_d = qk.shape
    d = two_d // 2
    n_heads = d // LANES
    n_kt, tk = vt.shape[1], vt.shape[3]
    span = _pick(n_kt, (DIFF_QK_SPAN, 1))
    tq = _pick(seq, (512, 256, 128))
    row = lambda a: a.reshape(1, -1).astype(F32)
    small = lambda n: pl.BlockSpec((1, n), lambda b, h, i: (0, 0))
    return pl.pallas_call(
        functools.partial(_diff_attn_kernel, lam_init=lam_init, span=span),
        grid=(bsz, n_heads, seq // tq),
        in_specs=[pl.BlockSpec((None, tq, LANES), lambda b, h, i: (b, i, h)),
                  pl.BlockSpec((None, seq, LANES), lambda b, h, i: (b, 0, n_heads + h)),
                  pl.BlockSpec((None, n_kt, LANES, tk), lambda b, h, i: (b, 0, h, 0)),
                  small(HEAD_DIM), small(HEAD_DIM), small(HEAD_DIM), small(HEAD_DIM),
                  small(LANES)],
        out_specs=pl.BlockSpec((None, tq, LANES), lambda b, h, i: (b, i, h)),
        out_shape=jax.ShapeDtypeStruct((bsz, seq, d), BF16),
        scratch_shapes=[pltpu.VMEM((2, span * tk, tq), F32), pltpu.VMEM((2, span * tk, tq), F32),
                        pltpu.VMEM((2, LANES + ONES_ROWS, tq), F32)],
        compiler_params=_cparams(("parallel", "parallel", "parallel")),
        name="diff_attention",
    )(qk, qk, vt, row(lq1), row(lk1), row(lq2), row(lk2), row(subln))


def _out_proj_kernel(*refs, n_in):
    ys = refs[:n_in]
    ws = refs[n_in:2 * n_in]
    x_ref, g_ref, o_ref = refs[2 * n_in:]
    m = jnp.dot(ys[0][...], ws[0][...], preferred_element_type=F32)
    for y_ref, w_ref in zip(ys[1:], ws[1:]):
        m = m + jnp.dot(y_ref[...], w_ref[...], preferred_element_type=F32)
    o_ref[...] = x_ref[...] + _rms(m, g_ref[...])


def out_proj_residual(ys, w, x, g):
    t, d = x.shape
    tm = _pick(t, (512, 256, 128))
    kdim = ys[0].shape[1]
    assert all(y.shape[1] == kdim for y in ys) and w.shape[0] == kdim * len(ys)
    in_specs = ([pl.BlockSpec((tm, kdim), lambda i: (i, 0)) for _ in ys]
                + [pl.BlockSpec((kdim, d), lambda i, r=r: (r, 0)) for r in range(len(ys))]
                + [pl.BlockSpec((tm, d), lambda i: (i, 0)), pl.BlockSpec((1, d), lambda i: (0, 0))])
    return pl.pallas_call(
        functools.partial(_out_proj_kernel, n_in=len(ys)),
        grid=(t // tm,),
        in_specs=in_specs,
        out_specs=pl.BlockSpec((tm, d), lambda i: (i, 0)),
        out_shape=jax.ShapeDtypeStruct((t, d), F32),
        compiler_params=_cparams(("parallel",)),
        name="out_proj",
    )(*ys, *([w] * len(ys)), x, g)


def _ffn_kernel(x_ref, gpre_ref, gpost_ref, wg_ref, wu_ref, wd_ref, o_ref, h_ref, acc_ref):
    f = pl.program_id(1)

    @pl.when(f == 0)
    def _():
        h_ref[...] = _rms(x_ref[...], gpre_ref[...]).astype(BF16)
        acc_ref[...] = jnp.zeros_like(acc_ref)

    h = h_ref[...]
    gate = jnp.dot(h, wg_ref[...], preferred_element_type=F32)
    up = jnp.dot(h, wu_ref[...], preferred_element_type=F32)
    act = (gate / (1.0 + jnp.exp(-gate))) * up
    acc_ref[...] += jnp.dot(act.astype(BF16), wd_ref[...], preferred_element_type=F32)

    @pl.when(f == pl.num_programs(1) - 1)
    def _():
        o_ref[...] = x_ref[...] + _rms(acc_ref[...], gpost_ref[...])


def ffn_residual(x, g_pre, g_post, w_gate, w_up, w_down):
    t, d = x.shape
    f = w_gate.shape[1]
    tm = _pick(t, (512, 256, 128))
    tf = _pick(f, (512, 256, 128))
    return pl.pallas_call(
        _ffn_kernel,
        grid=(t // tm, f // tf),
        in_specs=[pl.BlockSpec((tm, d), lambda i, j: (i, 0)),
                  pl.BlockSpec((1, d), lambda i, j: (0, 0)),
                  pl.BlockSpec((1, d), lambda i, j: (0, 0)),
                  pl.BlockSpec((d, tf), lambda i, j: (0, j)),
                  pl.BlockSpec((d, tf), lambda i, j: (0, j)),
                  pl.BlockSpec((tf, d), lambda i, j: (j, 0))],
        out_specs=pl.BlockSpec((tm, d), lambda i, j: (i, 0)),
        out_shape=jax.ShapeDtypeStruct((t, d), F32),
        scratch_shapes=[pltpu.VMEM((tm, d), BF16), pltpu.VMEM((tm, d), F32)],
        compiler_params=_cparams(("parallel", "arbitrary")),
        name="ffn",
    )(x, g_pre, g_post, w_gate, w_up, w_down)


def kernel(x, norm_mix_pre, norm_mix_post, norm_ffn_pre, norm_ffn_post, ev_w_in, ev_conv_w, ev_conv_b, ev_w_r, ev_b_r, ev_w_i, ev_b_i, ev_lam, ev_w_out, od_w_in, od_lam_q1, od_lam_k1, od_lam_q2, od_lam_k2, od_subln, od_w_out, ffn_w_gate, ffn_w_up, ffn_w_down):
    bsz, seq, d = x.shape
    depth = norm_mix_pre.shape[0]
    t = bsz * seq
    aw = ev_conv_w.shape[-1]
    rope = rope_tables(seq)
    gain = lambda g: g.reshape(1, d).astype(F32)
    ev_w_in, ev_w_out, od_w_in, od_w_out, ffn_w_gate, ffn_w_up, ffn_w_down = (
        w.astype(BF16) for w in (ev_w_in, ev_w_out, od_w_in, od_w_out,
                                 ffn_w_gate, ffn_w_up, ffn_w_down))
    xf = x.reshape(t, d)
    for layer in range(depth):
        j = layer // 2
        g_pre = gain(norm_mix_pre[layer])
        if layer % 2 == 0:
            w_in = ev_w_in[j]
            z_a = norm_matmul(xf, g_pre, w_in, 0, 2 * aw, F32).reshape(bsz, seq, 2 * aw)
            qkv = qkv_proj(xf, g_pre, w_in, 2 * aw, w_in.shape[1] - 2 * aw, rope, seq, 3)
            rows = lambda p: p.reshape(1, aw).astype(F32)
            h_fwd = None
            for direction in (0, 1):
                h_fwd = rglru_direction(
                    z_a, ev_conv_w[j].astype(F32), rows(ev_conv_b[j]),
                    gate_group_weights(ev_w_r[j, direction], ev_w_i[j, direction]),
                    rows(ev_b_r[j, direction]), rows(ev_b_i[j, direction]),
                    rows(ev_lam[j, direction]), h_fwd, reverse=direction == 1)
            y_a = h_fwd.reshape(t, aw)
            y_b = dilated_mixture(qkv.reshape(bsz, seq, -1)).reshape(t, -1)
            ys, w_out = [y_a, y_b], ev_w_out[j]
        else:
            lam_init = 0.8 - 0.6 * math.exp(-0.3 * layer)
            w_in = od_w_in[j]
            qk = qkv_proj(xf, g_pre, w_in, 0, 2 * d, rope, seq, 2)
            vt = norm_matmul_t(xf, g_pre, w_in, 2 * d, d, bsz, seq, _pick(seq, (DIFF_TK, LANES)))
            y = diff_attention(qk.reshape(bsz, seq, -1), vt, od_lam_q1[j], od_lam_k1[j],
                               od_lam_q2[j], od_lam_k2[j], od_subln[j], lam_init)
            ys, w_out = [y.reshape(t, -1)], od_w_out[j]
        xf = out_proj_residual(ys, w_out, xf, gain(norm_mix_post[layer]))
        xf = ffn_residual(xf, gain(norm_ffn_pre[layer]), gain(norm_ffn_post[layer]),
                          ffn_w_gate[layer], ffn_w_up[layer], ffn_w_down[layer])
    return xf.reshape(bsz, seq, d)
```

```python
import functools
import math

import jax
import jax.numpy as jnp
from jax import lax
from jax.experimental import pallas as pl
from jax.experimental.pallas import tpu as pltpu

F32 = jnp.float32
BF16 = jnp.bfloat16

EPS = 1e-6
NEG = -1e30
HEAD_DIM = 64
ROT_DIM = HEAD_DIM // 4
ROT_HALF = ROT_DIM // 2
ROPE_THETA = 500000.0
CONV_WIDTH = 4
RG_C = 8.0
DILATION_PAIRS = ((128, 1), (512, 4), (2048, 16))
GELU_C = math.sqrt(2.0 / math.pi)

LANES = 128
SUBLANES = 8
BF16_SUBLANES = 16
MXU_DIM = 256
V7X_VMEM_BYTES = 64 * 1024 * 1024
VMEM_LIMIT_BYTES = V7X_VMEM_BYTES * 7 // 8

DIFF_TK = MXU_DIM
DIFF_QK_SPAN = 4
ONES_ROWS = BF16_SUBLANES

LOG2E = math.log2(math.e)
Q_SCALE = HEAD_DIM ** -0.5 * LOG2E

_NT = (((1,), (1,)), ((), ()))


def _pick(n, prefs):
    for p in prefs:
        if n % p == 0:
            return p
    return n


def _cparams(sem):
    return pltpu.CompilerParams(dimension_semantics=sem, vmem_limit_bytes=VMEM_LIMIT_BYTES)


def _rms(x, g):
    return x * lax.rsqrt(jnp.mean(x * x, axis=-1, keepdims=True) + EPS) * g


def _norm_matmul_kernel(x_ref, g_ref, w_ref, o_ref):
    h = _rms(x_ref[...], g_ref[...]).astype(BF16)
    o_ref[...] = jnp.dot(h, w_ref[...], preferred_element_type=F32).astype(o_ref.dtype)


def _col_block(col0, tn):
    assert col0 % tn == 0
    return col0 // tn


def norm_matmul(x, g, w, col0, n, out_dtype):
    t, d = x.shape
    tm = _pick(t, (512, 256, 128))
    tn = _pick(n, (2048, 1024, 512, 256, 128))
    j0 = _col_block(col0, tn)
    return pl.pallas_call(
        _norm_matmul_kernel,
        grid=(t // tm, n // tn),
        in_specs=[pl.BlockSpec((tm, d), lambda i, j: (i, 0)),
                  pl.BlockSpec((1, d), lambda i, j: (0, 0)),
                  pl.BlockSpec((d, tn), lambda i, j: (0, j0 + j))],
        out_specs=pl.BlockSpec((tm, tn), lambda i, j: (i, j)),
        out_shape=jax.ShapeDtypeStruct((t, n), out_dtype),
        compiler_params=_cparams(("parallel", "parallel")),
        name="norm_matmul",
    )(x, g, w)


def _qkv_kernel(x_ref, g_ref, w_ref, c_ref, sa_ref, sb_ref, o_ref):
    h = _rms(x_ref[...], g_ref[...]).astype(BF16)
    acc = jnp.dot(h, w_ref[...], preferred_element_type=F32)
    c, sa, sb = c_ref[...], sa_ref[...], sb_ref[...]
    for cb in range(acc.shape[1] // LANES):
        a = acc[:, cb * LANES:(cb + 1) * LANES]
        r = a * c + pltpu.roll(a, LANES - ROT_HALF, 1) * sa + pltpu.roll(a, ROT_HALF, 1) * sb
        o_ref[:, cb * LANES:(cb + 1) * LANES] = r.astype(o_ref.dtype)


def qkv_proj(x, g, w, col0, n, rope, seq, n_parts):
    t, d = x.shape
    width = n // n_parts
    tm = _pick(seq, (512, 256, 128))
    tn = _pick(width, (2048, 1024, 512, 256, 128))
    per_part = width // tn
    j0 = _col_block(col0, tn)
    nblk_s = seq // tm
    tab_spec = pl.BlockSpec((None, tm, LANES), lambda i, j: (j // per_part, i % nblk_s, 0))
    return pl.pallas_call(
        _qkv_kernel,
        grid=(t // tm, n // tn),
        in_specs=[pl.BlockSpec((tm, d), lambda i, j: (i, 0)),
                  pl.BlockSpec((1, d), lambda i, j: (0, 0)),
                  pl.BlockSpec((d, tn), lambda i, j: (0, j0 + j)),
                  tab_spec, tab_spec, tab_spec],
        out_specs=pl.BlockSpec((tm, tn), lambda i, j: (i, j)),
        out_shape=jax.ShapeDtypeStruct((t, n), BF16),
        compiler_params=_cparams(("parallel", "parallel")),
        name="qkv_proj",
    )(x, g, w, *rope)


def rope_tables(seq):
    pos = jnp.arange(seq, dtype=F32)
    inv = ROPE_THETA ** (-jnp.arange(0, ROT_DIM, 2, dtype=F32) / ROT_DIM)
    ang = pos[:, None] * inv[None, :]
    cos, sin = jnp.cos(ang), jnp.sin(ang)
    zeros = jnp.zeros((seq, HEAD_DIM - ROT_DIM), F32)
    zh = jnp.zeros((seq, ROT_HALF), F32)
    c = jnp.concatenate([cos, cos, zeros + 1.0], axis=1)
    sa = jnp.concatenate([-sin, zh, zeros], axis=1)
    sb = jnp.concatenate([zh, sin, zeros], axis=1)
    rep = LANES // HEAD_DIM
    ident = (jnp.ones_like(c), jnp.zeros_like(c), jnp.zeros_like(c))
    return tuple(jnp.tile(jnp.stack([tab * Q_SCALE, tab, idt]), (1, 1, rep))
                 for tab, idt in zip((c, sa, sb), ident))


def _rglru_kernel(*refs, reverse, final, n_blk):
    if final:
        (xp_ref, xc_ref, xn_ref, cw_ref, cb_ref, wg_ref, br_ref, bi_ref, lam_ref,
         gate_ref, prev_ref, o_ref, carry_ref) = refs
    else:
        (xp_ref, xc_ref, xn_ref, cw_ref, cb_ref, wg_ref, br_ref, bi_ref, lam_ref,
         o_ref, carry_ref) = refs
    i = pl.program_id(1)
    blk = n_blk - 1 - i if reverse else i
    tb, aw = xc_ref.shape

    @pl.when(i == 0)
    def _():
        carry_ref[...] = jnp.zeros_like(carry_ref)

    x0 = xc_ref[...]
    xp = jnp.where(blk > 0, xp_ref[...], 0.0)
    xn = jnp.where(blk < n_blk - 1, xn_ref[...], 0.0)
    row = lax.broadcasted_iota(jnp.int32, (tb, aw), 0)
    zpad = jnp.zeros((tb - SUBLANES, aw), F32)
    conv = cb_ref[...] + cw_ref[2:3, :] * x0
    for back in (1, 2):
        head = jnp.concatenate([pltpu.roll(xp, back, 0), zpad], axis=0)
        shifted = jnp.where(row < back, head, pltpu.roll(x0, back, 0))
        conv = conv + cw_ref[2 - back:3 - back, :] * shifted
    tail = jnp.concatenate([zpad, pltpu.roll(xn, SUBLANES - 1, 0)], axis=0)
    shifted = jnp.where(row == tb - 1, tail, pltpu.roll(x0, tb - 1, 0))
    conv = conv + cw_ref[3:4, :] * shifted

    n_grp, gw, _ = wg_ref.shape
    cbf = conv.astype(BF16)
    zr, zi = [], []
    for g in range(n_grp):
        z = jnp.dot(cbf[:, g * gw:(g + 1) * gw], wg_ref[g], preferred_element_type=F32)
        zr.append(z[:, :gw])
        zi.append(z[:, gw:])
    zr = jnp.concatenate(zr, axis=1) if n_grp > 1 else zr[0]
    zi = jnp.concatenate(zi, axis=1) if n_grp > 1 else zi[0]
    r = 1.0 / (1.0 + jnp.exp(-(zr + br_ref[...])))
    ig = 1.0 / (1.0 + jnp.exp(-(zi + bi_ref[...])))
    nl = -lam_ref[...]
    softplus = jnp.maximum(nl, 0.0) + jnp.log(1.0 + jnp.exp(-jnp.abs(nl)))
    a = jnp.exp(-RG_C * r * softplus)
    u = jnp.sqrt(jnp.maximum(1.0 - a * a, 0.0)) * (ig * conv)

    n_grp8 = tb // SUBLANES
    a3 = a.reshape(n_grp8, SUBLANES, aw)
    u3 = u.reshape(n_grp8, SUBLANES, aw)
    sub = lax.broadcasted_iota(jnp.int32, a3.shape, 1)
    s = 1
    while s < SUBLANES:
        if reverse:
            keep = sub < SUBLANES - s
            a_sh = jnp.where(keep, pltpu.roll(a3, SUBLANES - s, 1), 1.0)
            u_sh = jnp.where(keep, pltpu.roll(u3, SUBLANES - s, 1), 0.0)
        else:
            keep = sub >= s
            a_sh = jnp.where(keep, pltpu.roll(a3, s, 1), 1.0)
            u_sh = jnp.where(keep, pltpu.roll(u3, s, 1), 0.0)
        u3 = a3 * u_sh + u3
        a3 = a3 * a_sh
        s *= 2
    edge = 0 if reverse else SUBLANES - 1
    state = carry_ref[edge:edge + 1, :]
    groups = [None] * n_grp8
    for j in (range(n_grp8 - 1, -1, -1) if reverse else range(n_grp8)):
        groups[j] = a3[j] * state + u3[j]
        state = groups[j][edge:edge + 1, :]
    h = jnp.concatenate(groups, axis=0)
    carry_ref[...] = groups[0] if reverse else groups[-1]

    if final:
        gt = gate_ref[...]
        gelu = 0.5 * gt * (1.0 + jnp.tanh(GELU_C * (gt + 0.044715 * (gt * gt * gt))))
        o_ref[...] = ((prev_ref[...] + h) * gelu).astype(o_ref.dtype)
    else:
        o_ref[...] = h


def rglru_direction(z_a, conv_w, conv_b, w_gate, b_r, b_i, lam, prev, reverse):
    bsz, seq, two_aw = z_a.shape
    aw = two_aw // 2
    assert conv_w.shape == (CONV_WIDTH, aw) and CONV_WIDTH == 4
    tb = _pick(seq, (256, 128, 64, 32, 16, 8))
    n_blk = seq // tb
    rows8 = seq // SUBLANES
    per8 = tb // SUBLANES
    final = prev is not None

    def blk(i):
        return n_blk - 1 - i if reverse else i

    xr_specs = [
        pl.BlockSpec((None, SUBLANES, aw), lambda b, i: (b, jnp.maximum(blk(i) * per8 - 1, 0), 1)),
        pl.BlockSpec((None, tb, aw), lambda b, i: (b, blk(i), 1)),
        pl.BlockSpec((None, SUBLANES, aw),
                     lambda b, i: (b, jnp.minimum((blk(i) + 1) * per8, rows8 - 1), 1)),
    ]
    full = lambda a: pl.BlockSpec(a.shape, lambda b, i: (0,) * a.ndim)
    params = [conv_w, conv_b, w_gate, b_r, b_i, lam]
    in_specs = xr_specs + [full(p) for p in params]
    args = [z_a, z_a, z_a] + params
    if final:
        in_specs += [pl.BlockSpec((None, tb, aw), lambda b, i: (b, blk(i), 0)),
                     pl.BlockSpec((None, tb, aw), lambda b, i: (b, blk(i), 0))]
        args += [z_a, prev]
    return pl.pallas_call(
        functools.partial(_rglru_kernel, reverse=reverse, final=final, n_blk=n_blk),
        grid=(bsz, n_blk),
        in_specs=in_specs,
        out_specs=pl.BlockSpec((None, tb, aw), lambda b, i: (b, blk(i), 0)),
        out_shape=jax.ShapeDtypeStruct((bsz, seq, aw), BF16 if final else F32),
        scratch_shapes=[pltpu.VMEM((SUBLANES, aw), F32)],
        compiler_params=_cparams(("parallel", "arbitrary")),
        name="rglru_bwd" if reverse else "rglru_fwd",
    )(*args)


def gate_group_weights(w_r, w_i):
    n, bs, _ = w_r.shape
    aw = n * bs
    gw = _pick(aw, (MXU_DIM, LANES))
    per = gw // bs
    n_grp = aw // gw
    eye = jnp.eye(per, dtype=w_r.dtype)

    def dense(w):
        wg = w.reshape(n_grp, per, bs, bs)
        return jnp.einsum('gaef,ab->gaebf', wg, eye).reshape(n_grp, gw, gw)

    return jnp.concatenate([dense(w_r), dense(w_i)], axis=2).astype(BF16)


DIL_TILE = 128
DIL_UNROLL = 16


def _dilated_kernel(q_ref, kp_ref, kc_ref, kn_ref, vp_ref, vc_ref, vn_ref, o_ref,
                    qbuf, kbuf, vbuf, so_ref, sl_ref, *, seq, half, dilations):
    tb = q_ref.shape[0]
    halo = kp_ref.shape[0]
    t0 = pl.program_id(1) * tb
    qbuf[...] = q_ref[...].astype(F32)
    for buf, prev, cur, nxt in ((kbuf, kp_ref, kc_ref, kn_ref), (vbuf, vp_ref, vc_ref, vn_ref)):
        buf[0:halo] = prev[...].astype(F32)
        buf[halo:halo + tb] = cur[...].astype(F32)
        buf[halo + tb:halo + tb + halo] = nxt[...].astype(F32)

    nk = DIL_TILE + 2 * half
    rowq = lax.broadcasted_iota(jnp.int32, (DIL_TILE, nk), 0)
    colk = lax.broadcasted_iota(jnp.int32, (DIL_TILE, nk), 1)
    band_bias = jnp.where(jnp.abs(colk - half - rowq) <= half, 0.0, NEG).astype(F32)
    col1 = lax.broadcasted_iota(jnp.int32, (1, nk), 1)
    lane = lax.broadcasted_iota(jnp.int32, (DIL_TILE, LANES), 1)
    lo = lane < HEAD_DIM
    lo_k = lax.broadcasted_iota(jnp.int32, (nk, LANES), 1) < HEAD_DIM

    for bi, d in enumerate(dilations):
        span = DIL_TILE * d

        def rows(ref, start, size, d=d):
            if d == 1:
                return ref[pl.ds(pl.multiple_of(start, SUBLANES), size), :]
            return ref[pl.ds(start, size, stride=d), :]

        def tile(n, carry, bi=bi, d=d, span=span, rows=rows):
            g, r = n // d, n % d
            qs = g * span + r
            ks = halo + qs - half * d
            tok = (t0 - halo + ks) + col1 * d
            bias = band_bias + jnp.where((tok >= 0) & (tok < seq), 0.0, NEG)
            q = rows(qbuf, qs, DIL_TILE).astype(BF16)
            k = rows(kbuf, ks, nk).astype(BF16)
            v = rows(vbuf, ks, nk).astype(BF16)
            res, ms = None, []
            for first in (True, False):
                own = lo if first else ~lo
                qm = jnp.where(own, q, jnp.zeros_like(q))
                s = lax.dot_general(qm, k, _NT, preferred_element_type=F32) + bias
                m = jnp.max(s, axis=1, keepdims=True)
                p = jnp.exp2((s - m).astype(BF16))
                own_k = lo_k if first else ~lo_k
                v_ext = jnp.concatenate(
                    [jnp.where(own_k, v, jnp.zeros_like(v)),
                     jnp.where(own_k, 1.0, 0.0).astype(BF16)], axis=1)
                pv = jnp.dot(p, v_ext, preferred_element_type=F32)
                res = pv if res is None else res + pv
                ms.append(m)
            den = res[:, LANES:]
            o = res[:, :LANES] / den
            lse = jnp.where(lo, ms[0], ms[1]) + jnp.log2(den)
            if d == 1:
                dst = pl.ds(pl.multiple_of(qs, SUBLANES), DIL_TILE)
            else:
                dst = pl.ds(qs, DIL_TILE, stride=d)
            if bi > 0:
                lp = sl_ref[dst, :]
                mx = jnp.maximum(lp, lse)
                tot = mx + jnp.log2(jnp.exp2(lp - mx) + jnp.exp2(lse - mx))
                o = so_ref[dst, :] * jnp.exp2(lp - tot) + o * jnp.exp2(lse - tot)
                lse = tot
            so_ref[dst, :] = o
            if bi < len(dilations) - 1:
                sl_ref[dst, :] = lse
            return carry

        lax.fori_loop(0, tb // DIL_TILE, tile, 0, unroll=DIL_UNROLL)

    o_ref[...] = so_ref[...].astype(o_ref.dtype)


def dilated_mixture(qkv):
    bsz, seq, three_w = qkv.shape
    width = three_w // 3
    n_pairs = width // LANES
    dilations = tuple(sorted((d for _, d in DILATION_PAIRS), reverse=True))
    halves = {w // (2 * d) for w, d in DILATION_PAIRS}
    assert len(halves) == 1
    half = halves.pop()
    halo = half * max(dilations)
    tb = DIL_TILE * max(dilations)
    assert seq % tb == 0 and tb % halo == 0 and all(max(dilations) % d == 0 for d in dilations)
    per = tb // halo
    n_halo = seq // halo

    def centre(col):
        return pl.BlockSpec((None, tb, LANES), lambda b, i, p: (b, i, col * n_pairs + p))

    def before(col):
        return pl.BlockSpec((None, halo, LANES),
                            lambda b, i, p: (b, jnp.maximum(i * per - 1, 0), col * n_pairs + p))

    def after(col):
        return pl.BlockSpec((None, halo, LANES),
                            lambda b, i, p: (b, jnp.minimum((i + 1) * per, n_halo - 1),
                                             col * n_pairs + p))

    return pl.pallas_call(
        functools.partial(_dilated_kernel, seq=seq, half=half, dilations=dilations),
        grid=(bsz, seq // tb, n_pairs),
        in_specs=[centre(0), before(1), centre(1), after(1), before(2), centre(2), after(2)],
        out_specs=pl.BlockSpec((None, tb, LANES), lambda b, i, p: (b, i, p)),
        out_shape=jax.ShapeDtypeStruct((bsz, seq, width), BF16),
        scratch_shapes=[pltpu.VMEM((tb, LANES), F32),
                        pltpu.VMEM((tb + 2 * halo, LANES), F32),
                        pltpu.VMEM((tb + 2 * halo, LANES), F32),
                        pltpu.VMEM((tb, LANES), F32),
                        pltpu.VMEM((tb, LANES), F32)],
        compiler_params=_cparams(("parallel", "parallel", "parallel")),
        name="dilated_mixture",
    )(*([qkv] * 7))


def _norm_matmul_t_kernel(x_ref, g_ref, w_ref, o_ref):
    h = _rms(x_ref[...], g_ref[...]).astype(BF16)
    acc_t = jnp.dot(h, w_ref[...], preferred_element_type=F32).T
    n_chunk, _, tk = o_ref.shape
    for c in range(n_chunk):
        o_ref[c] = acc_t[:, c * tk:(c + 1) * tk].astype(o_ref.dtype)


def norm_matmul_t(x, g, w, col0, n, bsz, seq, tk):
    t, d = x.shape
    tm = _pick(seq, (512, 256, 128))
    tn = _pick(n, (1024, 512, 256, 128))
    j0 = _col_block(col0, tn)
    assert tm % tk == 0
    per = tm // tk
    nblk_s = seq // tm
    return pl.pallas_call(
        _norm_matmul_t_kernel,
        grid=(t // tm, n // tn),
        in_specs=[pl.BlockSpec((tm, d), lambda i, j: (i, 0)),
                  pl.BlockSpec((1, d), lambda i, j: (0, 0)),
                  pl.BlockSpec((d, tn), lambda i, j: (0, j0 + j))],
        out_specs=pl.BlockSpec((None, per, tn, tk), lambda i, j: (i // nblk_s, i % nblk_s, j, 0)),
        out_shape=jax.ShapeDtypeStruct((bsz, seq // tk, n, tk), BF16),
        compiler_params=_cparams(("parallel", "parallel")),
        name="norm_matmul_t",
    )(x, g, w)


def _diff_attn_kernel(q_ref, k_ref, vt_ref, lq1_ref, lk1_ref, lq2_ref, lk2_ref, sub_ref, o_ref,
                      sa_ref, sb_ref, acc_ref, *, lam_init, span):
    tq = q_ref.shape[0]
    n_kt, _, tk = vt_ref.shape
    q = q_ref[...]
    lane = lax.broadcasted_iota(jnp.int32, (tq, LANES), 1)
    lo = lane < HEAD_DIM
    q1 = jnp.where(lo, q, jnp.zeros_like(q))
    q2 = jnp.where(lo, jnp.zeros_like(q), q)

    n_qt = tq // MXU_DIM if tq % MXU_DIM == 0 else 1
    qt = tq // n_qt
    q_tiles = [(q1[n * qt:(n + 1) * qt], q2[n * qt:(n + 1) * qt]) for n in range(n_qt)]
    pieces = [(n, j) for n in range(n_qt) for j in range(2)]
    ones = jnp.ones((ONES_ROWS, tk), BF16)
    n_sp = n_kt // span

    def score_piece(sp, s_ref, n, j):
        k = k_ref[pl.ds(sp * span * tk, span * tk), :]
        st = lax.dot_general(k, q_tiles[n][j], _NT, preferred_element_type=F32)
        s_ref[j, :, n * qt:(n + 1) * qt] = st
        return [jnp.max(st[h * tk:(h + 1) * tk], axis=0, keepdims=True) for h in range(span)]

    def softmax_piece(s_ref, h, n, j, m, cm):
        m_new = jnp.maximum(m, cm)
        alpha = jnp.exp2(m - m_new)
        st = s_ref[j, h * tk:(h + 1) * tk, n * qt:(n + 1) * qt]
        return m_new, alpha, jnp.exp2((st - m_new).astype(BF16))

    def pv_piece(vt, n, j, alpha, p):
        cols = slice(n * qt, (n + 1) * qt)
        acc_ref[j, :, cols] = alpha * acc_ref[j, :, cols] + jnp.dot(vt, p, preferred_element_type=F32)

    bufs = (sa_ref, sb_ref)
    acc_ref[...] = jnp.zeros_like(acc_ref)
    ms = {p: jnp.full((1, qt), NEG, F32) for p in pieces}
    cms = {p: score_piece(0, sa_ref, *p) for p in pieces}
    for sp in range(n_sp):
        vts = [jnp.concatenate([vt_ref[sp * span + h], ones], axis=0) for h in range(span)]
        cms_next = {}
        for n, j in pieces:
            for h in range(span):
                ms[n, j], alpha, p = softmax_piece(bufs[sp % 2], h, n, j, ms[n, j], cms[n, j][h])
                pv_piece(vts[h], n, j, alpha, p)
                if sp + 1 < n_sp and h == 0:
                    cms_next[n, j] = score_piece(sp + 1, bufs[(sp + 1) % 2], n, j)
        cms = cms_next
    a1, l1 = acc_ref[0, :LANES], acc_ref[0, LANES:LANES + 1]
    a2, l2 = acc_ref[1, :LANES], acc_ref[1, LANES:LANES + 1]

    lam = (jnp.exp(jnp.sum(lq1_ref[...] * lk1_ref[...], axis=1, keepdims=True))
           - jnp.exp(jnp.sum(lq2_ref[...] * lk2_ref[...], axis=1, keepdims=True)) + lam_init)
    o_t = a1 * (1.0 / l1) - lam * (a2 * (1.0 / l2))
    o_t = o_t * (lax.rsqrt(jnp.mean(o_t * o_t, axis=0, keepdims=True) + EPS) * (1.0 - lam_init))
    o_ref[...] = (o_t.T * sub_ref[...]).astype(o_ref.dtype)


def diff_attention(qk, vt, lq1, lk1, lq2, lk2, subln, lam_init):
    bsz, seq, two_d = qk.shape
    d = two_d // 2
    n_heads = d // LANES
    n_kt, tk = vt.shape[1], vt.shape[3]
    span = _pick(n_kt, (DIFF_QK_SPAN, 1))
    tq = _pick(seq, (512, 256, 128))
    row = lambda a: a.reshape(1, -1).astype(F32)
    small = lambda n: pl.BlockSpec((1, n), lambda b, h, i: (0, 0))
    return pl.pallas_call(
        functools.partial(_diff_attn_kernel, lam_init=lam_init, span=span),
        grid=(bsz, n_heads, seq // tq),
        in_specs=[pl.BlockSpec((None, tq, LANES), lambda b, h, i: (b, i, h)),
                  pl.BlockSpec((None, seq, LANES), lambda b, h, i: (b, 0, n_heads + h)),
                  pl.BlockSpec((None, n_kt, LANES, tk), lambda b, h, i: (b, 0, h, 0)),
                  small(HEAD_DIM), small(HEAD_DIM), small(HEAD_DIM), small(HEAD_DIM),
                  small(LANES)],
        out_specs=pl.BlockSpec((None, tq, LANES), lambda b, h, i: (b, i, h)),
        out_shape=jax.ShapeDtypeStruct((bsz, seq, d), BF16),
        scratch_shapes=[pltpu.VMEM((2, span * tk, tq), F32), pltpu.VMEM((2, span * tk, tq), F32),
                        pltpu.VMEM((2, LANES + ONES_ROWS, tq), F32)],
        compiler_params=_cparams(("parallel", "parallel", "parallel")),
        name="diff_attention",
    )(qk, qk, vt, row(lq1), row(lk1), row(lq2), row(lk2), row(subln))


def _out_proj_kernel(*refs, n_in):
    ys = refs[:n_in]
    ws = refs[n_in:2 * n_in]
    x_ref, g_ref, o_ref = refs[2 * n_in:]
    m = jnp.dot(ys[0][...], ws[0][...], preferred_element_type=F32)
    for y_ref, w_ref in zip(ys[1:], ws[1:]):
        m = m + jnp.dot(y_ref[...], w_ref[...], preferred_element_type=F32)
    o_ref[...] = x_ref[...] + _rms(m, g_ref[...])


def out_proj_residual(ys, w, x, g):
    t, d = x.shape
    tm = _pick(t, (512, 256, 128))
    kdim = ys[0].shape[1]
    assert all(y.shape[1] == kdim for y in ys) and w.shape[0] == kdim * len(ys)
    in_specs = ([pl.BlockSpec((tm, kdim), lambda i: (i, 0)) for _ in ys]
                + [pl.BlockSpec((kdim, d), lambda i, r=r: (r, 0)) for r in range(len(ys))]
                + [pl.BlockSpec((tm, d), lambda i: (i, 0)), pl.BlockSpec((1, d), lambda i: (0, 0))])
    return pl.pallas_call(
        functools.partial(_out_proj_kernel, n_in=len(ys)),
        grid=(t // tm,),
        in_specs=in_specs,
        out_specs=pl.BlockSpec((tm, d), lambda i: (i, 0)),
        out_shape=jax.ShapeDtypeStruct((t, d), F32),
        compiler_params=_cparams(("parallel",)),
        name="out_proj",
    )(*ys, *([w] * len(ys)), x, g)


def _ffn_kernel(x_ref, gpre_ref, gpost_ref, wg_ref, wu_ref, wd_ref, o_ref, h_ref, acc_ref):
    f = pl.program_id(1)

    @pl.when(f == 0)
    def _():
        h_ref[...] = _rms(x_ref[...], gpre_ref[...]).astype(BF16)
        acc_ref[...] = jnp.zeros_like(acc_ref)

    h = h_ref[...]
    gate = jnp.dot(h, wg_ref[...], preferred_element_type=F32)
    up = jnp.dot(h, wu_ref[...], preferred_element_type=F32)
    act = (gate / (1.0 + jnp.exp(-gate))) * up
    acc_ref[...] += jnp.dot(act.astype(BF16), wd_ref[...], preferred_element_type=F32)

    @pl.when(f == pl.num_programs(1) - 1)
    def _():
        o_ref[...] = x_ref[...] + _rms(acc_ref[...], gpost_ref[...])


def ffn_residual(x, g_pre, g_post, w_gate, w_up, w_down):
    t, d = x.shape
    f = w_gate.shape[1]
    tm = _pick(t, (512, 256, 128))
    tf = _pick(f, (512, 256, 128))
    return pl.pallas_call(
        _ffn_kernel,
        grid=(t // tm, f // tf),
        in_specs=[pl.BlockSpec((tm, d), lambda i, j: (i, 0)),
                  pl.BlockSpec((1, d), lambda i, j: (0, 0)),
                  pl.BlockSpec((1, d), lambda i, j: (0, 0)),
                  pl.BlockSpec((d, tf), lambda i, j: (0, j)),
                  pl.BlockSpec((d, tf), lambda i, j: (0, j)),
                  pl.BlockSpec((tf, d), lambda i, j: (j, 0))],
        out_specs=pl.BlockSpec((tm, d), lambda i, j: (i, 0)),
        out_shape=jax.ShapeDtypeStruct((t, d), F32),
        scratch_shapes=[pltpu.VMEM((tm, d), BF16), pltpu.VMEM((tm, d), F32)],
        compiler_params=_cparams(("parallel", "arbitrary")),
        name="ffn",
    )(x, g_pre, g_post, w_gate, w_up, w_down)


def kernel(x, norm_mix_pre, norm_mix_post, norm_ffn_pre, norm_ffn_post, ev_w_in, ev_conv_w, ev_conv_b, ev_w_r, ev_b_r, ev_w_i, ev_b_i, ev_lam, ev_w_out, od_w_in, od_lam_q1, od_lam_k1, od_lam_q2, od_lam_k2, od_subln, od_w_out, ffn_w_gate, ffn_w_up, ffn_w_down):
    bsz, seq, d = x.shape
    depth = norm_mix_pre.shape[0]
    t = bsz * seq
    aw = ev_conv_w.shape[-1]
    rope = rope_tables(seq)
    gain = lambda g: g.reshape(1, d).astype(F32)
    ev_w_in, ev_w_out, od_w_in, od_w_out, ffn_w_gate, ffn_w_up, ffn_w_down = (
        w.astype(BF16) for w in (ev_w_in, ev_w_out, od_w_in, od_w_out,
                                 ffn_w_gate, ffn_w_up, ffn_w_down))
    xf = x.reshape(t, d)
    for layer in range(depth):
        j = layer // 2
        g_pre = gain(norm_mix_pre[layer])
        if layer % 2 == 0:
            w_in = ev_w_in[j]
            z_a = norm_matmul(xf, g_pre, w_in, 0, 2 * aw, F32).reshape(bsz, seq, 2 * aw)
            qkv = qkv_proj(xf, g_pre, w_in, 2 * aw, w_in.shape[1] - 2 * aw, rope, seq, 3)
            rows = lambda p: p.reshape(1, aw).astype(F32)
            h_fwd = None
            for direction in (0, 1):
                h_fwd = rglru_direction(
                    z_a, ev_conv_w[j].astype(F32), rows(ev_conv_b[j]),
                    gate_group_weights(ev_w_r[j, direction], ev_w_i[j, direction]),
                    rows(ev_b_r[j, direction]), rows(ev_b_i[j, direction]),
                    rows(ev_lam[j, direction]), h_fwd, reverse=direction == 1)
            y_a = h_fwd.reshape(t, aw)
            y_b = dilated_mixture(qkv.reshape(bsz, seq, -1)).reshape(t, -1)
            ys, w_out = [y_a, y_b], ev_w_out[j]
        else:
            lam_init = 0.8 - 0.6 * math.exp(-0.3 * layer)
            w_in = od_w_in[j]
            qk = qkv_proj(xf, g_pre, w_in, 0, 2 * d, rope, seq, 2)
            vt = norm_matmul_t(xf, g_pre, w_in, 2 * d, d, bsz, seq, _pick(seq, (DIFF_TK, LANES)))
            y = diff_attention(qk.reshape(bsz, seq, -1), vt, od_lam_q1[j], od_lam_k1[j],
                               od_lam_q2[j], od_lam_k2[j], od_subln[j], lam_init)
            ys, w_out = [y.reshape(t, -1)], od_w_out[j]
        xf = out_proj_residual(ys, w_out, xf, gain(norm_mix_post[layer]))
        xf = ffn_residual(xf, gain(norm_ffn_pre[layer]), gain(norm_ffn_post[layer]),
                          ffn_w_gate[layer], ffn_w_up[layer], ffn_w_down[layer])
    return xf.reshape(bsz, seq, d)
```

```python
import functools
import math

import jax
import jax.numpy as jnp
from jax import lax
from jax.experimental import pallas as pl
from jax.experimental.pallas import tpu as pltpu

F32 = jnp.float32
BF16 = jnp.bfloat16

EPS = 1e-6
NEG = -1e30
HEAD_DIM = 64
ROT_DIM = HEAD_DIM // 4
ROT_HALF = ROT_DIM // 2
ROPE_THETA = 500000.0
CONV_WIDTH = 4
RG_C = 8.0
DILATION_PAIRS = ((128, 1), (512, 4), (2048, 16))
GELU_C = math.sqrt(2.0 / math.pi)

LANES = 128
SUBLANES = 8
BF16_SUBLANES = 16
MXU_DIM = 256
V7X_VMEM_BYTES = 64 * 1024 * 1024
VMEM_LIMIT_BYTES = V7X_VMEM_BYTES * 7 // 8

QKV_MAX_TN = 3072
DIFF_TK = MXU_DIM
DIFF_QK_SPAN = 4
ONES_ROWS = BF16_SUBLANES

LOG2E = math.log2(math.e)
Q_SCALE = HEAD_DIM ** -0.5 * LOG2E

_NT = (((1,), (1,)), ((), ()))


def _pick(n, prefs):
    for p in prefs:
        if n % p == 0:
            return p
    return n


def _cparams(sem):
    return pltpu.CompilerParams(dimension_semantics=sem, vmem_limit_bytes=VMEM_LIMIT_BYTES)


def _rms(x, g):
    return x * lax.rsqrt(jnp.mean(x * x, axis=-1, keepdims=True) + EPS) * g


def _norm_matmul_kernel(x_ref, g_ref, w_ref, o_ref):
    h = _rms(x_ref[...], g_ref[...]).astype(BF16)
    o_ref[...] = jnp.dot(h, w_ref[...], preferred_element_type=F32).astype(o_ref.dtype)


def _col_block(col0, tn):
    assert col0 % tn == 0
    return col0 // tn


def norm_matmul(x, g, w, col0, n, out_dtype):
    t, d = x.shape
    tm = _pick(t, (512, 256, 128))
    tn = _pick(n, (2048, 1024, 512, 256, 128))
    j0 = _col_block(col0, tn)
    return pl.pallas_call(
        _norm_matmul_kernel,
        grid=(t // tm, n // tn),
        in_specs=[pl.BlockSpec((tm, d), lambda i, j: (i, 0)),
                  pl.BlockSpec((1, d), lambda i, j: (0, 0)),
                  pl.BlockSpec((d, tn), lambda i, j: (0, j0 + j))],
        out_specs=pl.BlockSpec((tm, tn), lambda i, j: (i, j)),
        out_shape=jax.ShapeDtypeStruct((t, n), out_dtype),
        compiler_params=_cparams(("parallel", "parallel")),
        name="norm_matmul",
    )(x, g, w)


def _qkv_kernel(x_ref, g_ref, w_ref, c_ref, sa_ref, sb_ref, o_ref, *, width):
    h = _rms(x_ref[...], g_ref[...]).astype(BF16)
    acc = jnp.dot(h, w_ref[...], preferred_element_type=F32)
    tn = acc.shape[1]
    col0 = pl.program_id(1) * tn
    for cb in range(tn // LANES):
        part = (col0 + cb * LANES) // width
        a = acc[:, cb * LANES:(cb + 1) * LANES]
        r = (a * c_ref[part] + pltpu.roll(a, LANES - ROT_HALF, 1) * sa_ref[part]
             + pltpu.roll(a, ROT_HALF, 1) * sb_ref[part])
        o_ref[:, cb * LANES:(cb + 1) * LANES] = r.astype(o_ref.dtype)


def qkv_proj(x, g, w, col0, n, rope, seq, n_parts):
    t, d = x.shape
    width = n // n_parts
    tm = _pick(seq, (512, 256, 128))
    tn = n if n <= QKV_MAX_TN else _pick(width, (2048, 1024, 512, 256, 128))
    j0 = _col_block(col0, tn)
    nblk_s = seq // tm
    tab_spec = pl.BlockSpec((3, tm, LANES), lambda i, j: (0, i % nblk_s, 0))
    return pl.pallas_call(
        functools.partial(_qkv_kernel, width=width),
        grid=(t // tm, n // tn),
        in_specs=[pl.BlockSpec((tm, d), lambda i, j: (i, 0)),
                  pl.BlockSpec((1, d), lambda i, j: (0, 0)),
                  pl.BlockSpec((d, tn), lambda i, j: (0, j0 + j)),
                  tab_spec, tab_spec, tab_spec],
        out_specs=pl.BlockSpec((tm, tn), lambda i, j: (i, j)),
        out_shape=jax.ShapeDtypeStruct((t, n), BF16),
        compiler_params=_cparams(("parallel", "parallel")),
        name="qkv_proj",
    )(x, g, w, *rope)


def rope_tables(seq):
    pos = jnp.arange(seq, dtype=F32)
    inv = ROPE_THETA ** (-jnp.arange(0, ROT_DIM, 2, dtype=F32) / ROT_DIM)
    ang = pos[:, None] * inv[None, :]
    cos, sin = jnp.cos(ang), jnp.sin(ang)
    zeros = jnp.zeros((seq, HEAD_DIM - ROT_DIM), F32)
    zh = jnp.zeros((seq, ROT_HALF), F32)
    c = jnp.concatenate([cos, cos, zeros + 1.0], axis=1)
    sa = jnp.concatenate([-sin, zh, zeros], axis=1)
    sb = jnp.concatenate([zh, sin, zeros], axis=1)
    rep = LANES // HEAD_DIM
    ident = (jnp.ones_like(c), jnp.zeros_like(c), jnp.zeros_like(c))
    return tuple(jnp.tile(jnp.stack([tab * Q_SCALE, tab, idt]), (1, 1, rep))
                 for tab, idt in zip((c, sa, sb), ident))


def _rglru_kernel(*refs, reverse, final, n_blk):
    if final:
        (xp_ref, xc_ref, xn_ref, cw_ref, cb_ref, wg_ref, br_ref, bi_ref, lam_ref,
         gate_ref, prev_ref, o_ref, carry_ref) = refs
    else:
        (xp_ref, xc_ref, xn_ref, cw_ref, cb_ref, wg_ref, br_ref, bi_ref, lam_ref,
         o_ref, carry_ref) = refs
    i = pl.program_id(1)
    blk = n_blk - 1 - i if reverse else i
    tb, aw = xc_ref.shape

    @pl.when(i == 0)
    def _():
        carry_ref[...] = jnp.zeros_like(carry_ref)

    x0 = xc_ref[...]
    xp = jnp.where(blk > 0, xp_ref[...], 0.0)
    xn = jnp.where(blk < n_blk - 1, xn_ref[...], 0.0)
    row = lax.broadcasted_iota(jnp.int32, (tb, aw), 0)
    zpad = jnp.zeros((tb - SUBLANES, aw), F32)
    conv = cb_ref[...] + cw_ref[2:3, :] * x0
    for back in (1, 2):
        head = jnp.concatenate([pltpu.roll(xp, back, 0), zpad], axis=0)
        shifted = jnp.where(row < back, head, pltpu.roll(x0, back, 0))
        conv = conv + cw_ref[2 - back:3 - back, :] * shifted
    tail = jnp.concatenate([zpad, pltpu.roll(xn, SUBLANES - 1, 0)], axis=0)
    shifted = jnp.where(row == tb - 1, tail, pltpu.roll(x0, tb - 1, 0))
    conv = conv + cw_ref[3:4, :] * shifted

    n_grp, gw, _ = wg_ref.shape
    cbf = conv.astype(BF16)
    zr, zi = [], []
    for g in range(n_grp):
        z = jnp.dot(cbf[:, g * gw:(g + 1) * gw], wg_ref[g], preferred_element_type=F32)
        zr.append(z[:, :gw])
        zi.append(z[:, gw:])
    zr = jnp.concatenate(zr, axis=1) if n_grp > 1 else zr[0]
    zi = jnp.concatenate(zi, axis=1) if n_grp > 1 else zi[0]
    r = 1.0 / (1.0 + jnp.exp(-(zr + br_ref[...])))
    ig = 1.0 / (1.0 + jnp.exp(-(zi + bi_ref[...])))
    nl = -lam_ref[...]
    softplus = jnp.maximum(nl, 0.0) + jnp.log(1.0 + jnp.exp(-jnp.abs(nl)))
    a = jnp.exp(-RG_C * r * softplus)
    u = jnp.sqrt(jnp.maximum(1.0 - a * a, 0.0)) * (ig * conv)

    n_grp8 = tb // SUBLANES
    a3 = a.reshape(n_grp8, SUBLANES, aw)
    u3 = u.reshape(n_grp8, SUBLANES, aw)
    sub = lax.broadcasted_iota(jnp.int32, a3.shape, 1)
    s = 1
    while s < SUBLANES:
        if reverse:
            keep = sub < SUBLANES - s
            a_sh = jnp.where(keep, pltpu.roll(a3, SUBLANES - s, 1), 1.0)
            u_sh = jnp.where(keep, pltpu.roll(u3, SUBLANES - s, 1), 0.0)
        else:
            keep = sub >= s
            a_sh = jnp.where(keep, pltpu.roll(a3, s, 1), 1.0)
            u_sh = jnp.where(keep, pltpu.roll(u3, s, 1), 0.0)
        u3 = a3 * u_sh + u3
        a3 = a3 * a_sh
        s *= 2
    edge = 0 if reverse else SUBLANES - 1
    state = carry_ref[edge:edge + 1, :]
    groups = [None] * n_grp8
    for j in (range(n_grp8 - 1, -1, -1) if reverse else range(n_grp8)):
        groups[j] = a3[j] * state + u3[j]
        state = groups[j][edge:edge + 1, :]
    h = jnp.concatenate(groups, axis=0)
    carry_ref[...] = groups[0] if reverse else groups[-1]

    if final:
        gt = gate_ref[...]
        gelu = 0.5 * gt * (1.0 + jnp.tanh(GELU_C * (gt + 0.044715 * (gt * gt * gt))))
        o_ref[...] = ((prev_ref[...] + h) * gelu).astype(o_ref.dtype)
    else:
        o_ref[...] = h


def rglru_direction(z_a, conv_w, conv_b, w_gate, b_r, b_i, lam, prev, reverse):
    bsz, seq, two_aw = z_a.shape
    aw = two_aw // 2
    assert conv_w.shape == (CONV_WIDTH, aw) and CONV_WIDTH == 4
    tb = _pick(seq, (256, 128, 64, 32, 16, 8))
    n_blk = seq // tb
    rows8 = seq // SUBLANES
    per8 = tb // SUBLANES
    final = prev is not None

    def blk(i):
        return n_blk - 1 - i if reverse else i

    xr_specs = [
        pl.BlockSpec((None, SUBLANES, aw), lambda b, i: (b, jnp.maximum(blk(i) * per8 - 1, 0), 1)),
        pl.BlockSpec((None, tb, aw), lambda b, i: (b, blk(i), 1)),
        pl.BlockSpec((None, SUBLANES, aw),
                     lambda b, i: (b, jnp.minimum((blk(i) + 1) * per8, rows8 - 1), 1)),
    ]
    full = lambda a: pl.BlockSpec(a.shape, lambda b, i: (0,) * a.ndim)
    params = [conv_w, conv_b, w_gate, b_r, b_i, lam]
    in_specs = xr_specs + [full(p) for p in params]
    args = [z_a, z_a, z_a] + params
    if final:
        in_specs += [pl.BlockSpec((None, tb, aw), lambda b, i: (b, blk(i), 0)),
                     pl.BlockSpec((None, tb, aw), lambda b, i: (b, blk(i), 0))]
        args += [z_a, prev]
    return pl.pallas_call(
        functools.partial(_rglru_kernel, reverse=reverse, final=final, n_blk=n_blk),
        grid=(bsz, n_blk),
        in_specs=in_specs,
        out_specs=pl.BlockSpec((None, tb, aw), lambda b, i: (b, blk(i), 0)),
        out_shape=jax.ShapeDtypeStruct((bsz, seq, aw), BF16 if final else F32),
        scratch_shapes=[pltpu.VMEM((SUBLANES, aw), F32)],
        compiler_params=_cparams(("parallel", "arbitrary")),
        name="rglru_bwd" if reverse else "rglru_fwd",
    )(*args)


def gate_group_weights(w_r, w_i):
    n, bs, _ = w_r.shape
    aw = n * bs
    gw = _pick(aw, (MXU_DIM, LANES))
    per = gw // bs
    n_grp = aw // gw
    eye = jnp.eye(per, dtype=w_r.dtype)

    def dense(w):
        wg = w.reshape(n_grp, per, bs, bs)
        return jnp.einsum('gaef,ab->gaebf', wg, eye).reshape(n_grp, gw, gw)

    return jnp.concatenate([dense(w_r), dense(w_i)], axis=2).astype(BF16)


DIL_TILE = 128
DIL_UNROLL = 16


def _dilated_kernel(q_ref, kp_ref, kc_ref, kn_ref, vp_ref, vc_ref, vn_ref, o_ref,
                    qbuf, kbuf, vbuf, so_ref, sl_ref, *, seq, half, dilations):
    tb = q_ref.shape[0]
    halo = kp_ref.shape[0]
    t0 = pl.program_id(1) * tb
    qbuf[...] = q_ref[...].astype(F32)
    for buf, prev, cur, nxt in ((kbuf, kp_ref, kc_ref, kn_ref), (vbuf, vp_ref, vc_ref, vn_ref)):
        buf[0:halo] = prev[...].astype(F32)
        buf[halo:halo + tb] = cur[...].astype(F32)
        buf[halo + tb:halo + tb + halo] = nxt[...].astype(F32)

    nk = DIL_TILE + 2 * half
    rowq = lax.broadcasted_iota(jnp.int32, (DIL_TILE, nk), 0)
    colk = lax.broadcasted_iota(jnp.int32, (DIL_TILE, nk), 1)
    band_bias = jnp.where(jnp.abs(colk - half - rowq) <= half, 0.0, NEG).astype(F32)
    col1 = lax.broadcasted_iota(jnp.int32, (1, nk), 1)
    lane = lax.broadcasted_iota(jnp.int32, (DIL_TILE, LANES), 1)
    lo = lane < HEAD_DIM
    lo_k = lax.broadcasted_iota(jnp.int32, (nk, LANES), 1) < HEAD_DIM

    for bi, d in enumerate(dilations):
        span = DIL_TILE * d

        def rows(ref, start, size, d=d):
            if d == 1:
                return ref[pl.ds(pl.multiple_of(start, SUBLANES), size), :]
            return ref[pl.ds(start, size, stride=d), :]

        def tile(n, carry, bi=bi, d=d, span=span, rows=rows):
            g, r = n // d, n % d
            qs = g * span + r
            ks = halo + qs - half * d
            tok = (t0 - halo + ks) + col1 * d
            bias = band_bias + jnp.where((tok >= 0) & (tok < seq), 0.0, NEG)
            q = rows(qbuf, qs, DIL_TILE).astype(BF16)
            k = rows(kbuf, ks, nk).astype(BF16)
            v = rows(vbuf, ks, nk).astype(BF16)
            res, ms = None, []
            for first in (True, False):
                own = lo if first else ~lo
                qm = jnp.where(own, q, jnp.zeros_like(q))
                s = lax.dot_general(qm, k, _NT, preferred_element_type=F32) + bias
                m = jnp.max(s, axis=1, keepdims=True)
                p = jnp.exp2((s - m).astype(BF16))
                own_k = lo_k if first else ~lo_k
                v_ext = jnp.concatenate(
                    [jnp.where(own_k, v, jnp.zeros_like(v)),
                     jnp.where(own_k, 1.0, 0.0).astype(BF16)], axis=1)
                pv = jnp.dot(p, v_ext, preferred_element_type=F32)
                res = pv if res is None else res + pv
                ms.append(m)
            den = res[:, LANES:]
            o = res[:, :LANES] / den
            lse = jnp.where(lo, ms[0], ms[1]) + jnp.log2(den)
            if d == 1:
                dst = pl.ds(pl.multiple_of(qs, SUBLANES), DIL_TILE)
            else:
                dst = pl.ds(qs, DIL_TILE, stride=d)
            if bi > 0:
                lp = sl_ref[dst, :]
                mx = jnp.maximum(lp, lse)
                tot = mx + jnp.log2(jnp.exp2(lp - mx) + jnp.exp2(lse - mx))
                o = so_ref[dst, :] * jnp.exp2(lp - tot) + o * jnp.exp2(lse - tot)
                lse = tot
            so_ref[dst, :] = o
            if bi < len(dilations) - 1:
                sl_ref[dst, :] = lse
            return carry

        lax.fori_loop(0, tb // DIL_TILE, tile, 0, unroll=DIL_UNROLL)

    o_ref[...] = so_ref[...].astype(o_ref.dtype)


def dilated_mixture(qkv):
    bsz, seq, three_w = qkv.shape
    width = three_w // 3
    n_pairs = width // LANES
    dilations = tuple(sorted((d for _, d in DILATION_PAIRS), reverse=True))
    halves = {w // (2 * d) for w, d in DILATION_PAIRS}
    assert len(halves) == 1
    half = halves.pop()
    halo = half * max(dilations)
    tb = DIL_TILE * max(dilations)
    assert seq % tb == 0 and tb % halo == 0 and all(max(dilations) % d == 0 for d in dilations)
    per = tb // halo
    n_halo = seq // halo

    def centre(col):
        return pl.BlockSpec((None, tb, LANES), lambda b, i, p: (b, i, col * n_pairs + p))

    def before(col):
        return pl.BlockSpec((None, halo, LANES),
                            lambda b, i, p: (b, jnp.maximum(i * per - 1, 0), col * n_pairs + p))

    def after(col):
        return pl.BlockSpec((None, halo, LANES),
                            lambda b, i, p: (b, jnp.minimum((i + 1) * per, n_halo - 1),
                                             col * n_pairs + p))

    return pl.pallas_call(
        functools.partial(_dilated_kernel, seq=seq, half=half, dilations=dilations),
        grid=(bsz, seq // tb, n_pairs),
        in_specs=[centre(0), before(1), centre(1), after(1), before(2), centre(2), after(2)],
        out_specs=pl.BlockSpec((None, tb, LANES), lambda b, i, p: (b, i, p)),
        out_shape=jax.ShapeDtypeStruct((bsz, seq, width), BF16),
        scratch_shapes=[pltpu.VMEM((tb, LANES), F32),
                        pltpu.VMEM((tb + 2 * halo, LANES), F32),
                        pltpu.VMEM((tb + 2 * halo, LANES), F32),
                        pltpu.VMEM((tb, LANES), F32),
                        pltpu.VMEM((tb, LANES), F32)],
        compiler_params=_cparams(("parallel", "parallel", "parallel")),
        name="dilated_mixture",
    )(*([qkv] * 7))


def _norm_matmul_t_kernel(x_ref, g_ref, w_ref, o_ref):
    h = _rms(x_ref[...], g_ref[...]).astype(BF16)
    acc_t = jnp.dot(h, w_ref[...], preferred_element_type=F32).T
    n_chunk, _, tk = o_ref.shape
    for c in range(n_chunk):
        o_ref[c] = acc_t[:, c * tk:(c + 1) * tk].astype(o_ref.dtype)


def norm_matmul_t(x, g, w, col0, n, bsz, seq, tk):
    t, d = x.shape
    tm = _pick(seq, (512, 256, 128))
    tn = _pick(n, (2048, 1024, 512, 256, 128))
    j0 = _col_block(col0, tn)
    assert tm % tk == 0
    per = tm // tk
    nblk_s = seq // tm
    return pl.pallas_call(
        _norm_matmul_t_kernel,
        grid=(t // tm, n // tn),
        in_specs=[pl.BlockSpec((tm, d), lambda i, j: (i, 0)),
                  pl.BlockSpec((1, d), lambda i, j: (0, 0)),
                  pl.BlockSpec((d, tn), lambda i, j: (0, j0 + j))],
        out_specs=pl.BlockSpec((None, per, tn, tk), lambda i, j: (i // nblk_s, i % nblk_s, j, 0)),
        out_shape=jax.ShapeDtypeStruct((bsz, seq // tk, n, tk), BF16),
        compiler_params=_cparams(("parallel", "parallel")),
        name="norm_matmul_t",
    )(x, g, w)


def _diff_attn_kernel(q_ref, k_ref, vt_ref, lq1_ref, lk1_ref, lq2_ref, lk2_ref, sub_ref, o_ref,
                      sa_ref, sb_ref, acc_ref, *, lam_init, span):
    tq = q_ref.shape[0]
    n_kt, _, tk = vt_ref.shape
    q = q_ref[...]
    lane = lax.broadcasted_iota(jnp.int32, (tq, LANES), 1)
    lo = lane < HEAD_DIM
    q1 = jnp.where(lo, q, jnp.zeros_like(q))
    q2 = jnp.where(lo, jnp.zeros_like(q), q)

    n_qt = tq // MXU_DIM if tq % MXU_DIM == 0 else 1
    qt = tq // n_qt
    q_tiles = [(q1[n * qt:(n + 1) * qt], q2[n * qt:(n + 1) * qt]) for n in range(n_qt)]
    pieces = [(n, j) for n in range(n_qt) for j in range(2)]
    ones = jnp.ones((ONES_ROWS, tk), BF16)
    n_sp = n_kt // span

    def score_piece(sp, s_ref, n, j):
        k = k_ref[pl.ds(sp * span * tk, span * tk), :]
        st = lax.dot_general(k, q_tiles[n][j], _NT, preferred_element_type=F32)
        s_ref[j, :, n * qt:(n + 1) * qt] = st
        return [jnp.max(st[h * tk:(h + 1) * tk], axis=0, keepdims=True) for h in range(span)]

    def softmax_piece(s_ref, h, n, j, m, cm):
        m_new = jnp.maximum(m, cm)
        alpha = jnp.exp2(m - m_new)
        st = s_ref[j, h * tk:(h + 1) * tk, n * qt:(n + 1) * qt]
        return m_new, alpha, jnp.exp2((st - m_new).astype(BF16))

    def pv_piece(vt, n, j, alpha, p):
        cols = slice(n * qt, (n + 1) * qt)
        acc_ref[j, :, cols] = alpha * acc_ref[j, :, cols] + jnp.dot(vt, p, preferred_element_type=F32)

    bufs = (sa_ref, sb_ref)
    acc_ref[...] = jnp.zeros_like(acc_ref)
    ms = {p: jnp.full((1, qt), NEG, F32) for p in pieces}
    cms = {p: score_piece(0, sa_ref, *p) for p in pieces}
    for sp in range(n_sp):
        vts = [jnp.concatenate([vt_ref[sp * span + h], ones], axis=0) for h in range(span)]
        cms_next = {}
        for n, j in pieces:
            for h in range(span):
                ms[n, j], alpha, p = softmax_piece(bufs[sp % 2], h, n, j, ms[n, j], cms[n, j][h])
                pv_piece(vts[h], n, j, alpha, p)
                if sp + 1 < n_sp and h == 0:
                    cms_next[n, j] = score_piece(sp + 1, bufs[(sp + 1) % 2], n, j)
        cms = cms_next
    a1, l1 = acc_ref[0, :LANES], acc_ref[0, LANES:LANES + 1]
    a2, l2 = acc_ref[1, :LANES], acc_ref[1, LANES:LANES + 1]

    lam = (jnp.exp(jnp.sum(lq1_ref[...] * lk1_ref[...], axis=1, keepdims=True))
           - jnp.exp(jnp.sum(lq2_ref[...] * lk2_ref[...], axis=1, keepdims=True)) + lam_init)
    o_t = a1 * (1.0 / l1) - lam * (a2 * (1.0 / l2))
    o_t = o_t * (lax.rsqrt(jnp.mean(o_t * o_t, axis=0, keepdims=True) + EPS) * (1.0 - lam_init))
    o_ref[...] = (o_t.T * sub_ref[...]).astype(o_ref.dtype)


def diff_attention(qk, vt, lq1, lk1, lq2, lk2, subln, lam_init):
    bsz, seq, two_d = qk.shape
    d = two_d // 2
    n_heads = d // LANES
    n_kt, tk = vt.shape[1], vt.shape[3]
    span = _pick(n_kt, (DIFF_QK_SPAN, 1))
    tq = _pick(seq, (512, 256, 128))
    row = lambda a: a.reshape(1, -1).astype(F32)
    small = lambda n: pl.BlockSpec((1, n), lambda b, h, i: (0, 0))
    return pl.pallas_call(
        functools.partial(_diff_attn_kernel, lam_init=lam_init, span=span),
        grid=(bsz, n_heads, seq // tq),
        in_specs=[pl.BlockSpec((None, tq, LANES), lambda b, h, i: (b, i, h)),
                  pl.BlockSpec((None, seq, LANES), lambda b, h, i: (b, 0, n_heads + h)),
                  pl.BlockSpec((None, n_kt, LANES, tk), lambda b, h, i: (b, 0, h, 0)),
                  small(HEAD_DIM), small(HEAD_DIM), small(HEAD_DIM), small(HEAD_DIM),
                  small(LANES)],
        out_specs=pl.BlockSpec((None, tq, LANES), lambda b, h, i: (b, i, h)),
        out_shape=jax.ShapeDtypeStruct((bsz, seq, d), BF16),
        scratch_shapes=[pltpu.VMEM((2, span * tk, tq), F32), pltpu.VMEM((2, span * tk, tq), F32),
                        pltpu.VMEM((2, LANES + ONES_ROWS, tq), F32)],
        compiler_params=_cparams(("parallel", "parallel", "parallel")),
        name="diff_attention",
    )(qk, qk, vt, row(lq1), row(lk1), row(lq2), row(lk2), row(subln))


def _out_proj_kernel(*refs, n_in):
    ys = refs[:n_in]
    ws = refs[n_in:2 * n_in]
    x_ref, g_ref, o_ref = refs[2 * n_in:]
    m = jnp.dot(ys[0][...], ws[0][...], preferred_element_type=F32)
    for y_ref, w_ref in zip(ys[1:], ws[1:]):
        m = m + jnp.dot(y_ref[...], w_ref[...], preferred_element_type=F32)
    o_ref[...] = x_ref[...] + _rms(m, g_ref[...])


def out_proj_residual(ys, w, x, g):
    t, d = x.shape
    tm = _pick(t, (512, 256, 128))
    kdim = ys[0].shape[1]
    assert all(y.shape[1] == kdim for y in ys) and w.shape[0] == kdim * len(ys)
    in_specs = ([pl.BlockSpec((tm, kdim), lambda i: (i, 0)) for _ in ys]
                + [pl.BlockSpec((kdim, d), lambda i, r=r: (r, 0)) for r in range(len(ys))]
                + [pl.BlockSpec((tm, d), lambda i: (i, 0)), pl.BlockSpec((1, d), lambda i: (0, 0))])
    return pl.pallas_call(
        functools.partial(_out_proj_kernel, n_in=len(ys)),
        grid=(t // tm,),
        in_specs=in_specs,
        out_specs=pl.BlockSpec((tm, d), lambda i: (i, 0)),
        out_shape=jax.ShapeDtypeStruct((t, d), F32),
        compiler_params=_cparams(("parallel",)),
        name="out_proj",
    )(*ys, *([w] * len(ys)), x, g)


def _ffn_kernel(x_ref, gpre_ref, gpost_ref, wg_ref, wu_ref, wd_ref, o_ref, h_ref, acc_ref):
    f = pl.program_id(1)

    @pl.when(f == 0)
    def _():
        h_ref[...] = _rms(x_ref[...], gpre_ref[...]).astype(BF16)
        acc_ref[...] = jnp.zeros_like(acc_ref)

    h = h_ref[...]
    gate = jnp.dot(h, wg_ref[...], preferred_element_type=F32)
    up = jnp.dot(h, wu_ref[...], preferred_element_type=F32)
    act = (gate / (1.0 + jnp.exp(-gate))) * up
    acc_ref[...] += jnp.dot(act.astype(BF16), wd_ref[...], preferred_element_type=F32)

    @pl.when(f == pl.num_programs(1) - 1)
    def _():
        o_ref[...] = x_ref[...] + _rms(acc_ref[...], gpost_ref[...])


def ffn_residual(x, g_pre, g_post, w_gate, w_up, w_down):
    t, d = x.shape
    f = w_gate.shape[1]
    tm = _pick(t, (512, 256, 128))
    tf = _pick(f, (512, 256, 128))
    return pl.pallas_call(
        _ffn_kernel,
        grid=(t // tm, f // tf),
        in_specs=[pl.BlockSpec((tm, d), lambda i, j: (i, 0)),
                  pl.BlockSpec((1, d), lambda i, j: (0, 0)),
                  pl.BlockSpec((1, d), lambda i, j: (0, 0)),
                  pl.BlockSpec((d, tf), lambda i, j: (0, j)),
                  pl.BlockSpec((d, tf), lambda i, j: (0, j)),
                  pl.BlockSpec((tf, d), lambda i, j: (j, 0))],
        out_specs=pl.BlockSpec((tm, d), lambda i, j: (i, 0)),
        out_shape=jax.ShapeDtypeStruct((t, d), F32),
        scratch_shapes=[pltpu.VMEM((tm, d), BF16), pltpu.VMEM((tm, d), F32)],
        compiler_params=_cparams(("parallel", "arbitrary")),
        name="ffn",
    )(x, g_pre, g_post, w_gate, w_up, w_down)


def kernel(x, norm_mix_pre, norm_mix_post, norm_ffn_pre, norm_ffn_post, ev_w_in, ev_conv_w, ev_conv_b, ev_w_r, ev_b_r, ev_w_i, ev_b_i, ev_lam, ev_w_out, od_w_in, od_lam_q1, od_lam_k1, od_lam_q2, od_lam_k2, od_subln, od_w_out, ffn_w_gate, ffn_w_up, ffn_w_down):
    bsz, seq, d = x.shape
    depth = norm_mix_pre.shape[0]
    t = bsz * seq
    aw = ev_conv_w.shape[-1]
    rope = rope_tables(seq)
    gain = lambda g: g.reshape(1, d).astype(F32)
    ev_w_in, ev_w_out, od_w_in, od_w_out, ffn_w_gate, ffn_w_up, ffn_w_down = (
        w.astype(BF16) for w in (ev_w_in, ev_w_out, od_w_in, od_w_out,
                                 ffn_w_gate, ffn_w_up, ffn_w_down))
    xf = x.reshape(t, d)
    for layer in range(depth):
        j = layer // 2
        g_pre = gain(norm_mix_pre[layer])
        if layer % 2 == 0:
            w_in = ev_w_in[j]
            z_a = norm_matmul(xf, g_pre, w_in, 0, 2 * aw, F32).reshape(bsz, seq, 2 * aw)
            qkv = qkv_proj(xf, g_pre, w_in[:, 2 * aw:], 0, w_in.shape[1] - 2 * aw, rope, seq, 3)
            rows = lambda p: p.reshape(1, aw).astype(F32)
            h_fwd = None
            for direction in (0, 1):
                h_fwd = rglru_direction(
                    z_a, ev_conv_w[j].astype(F32), rows(ev_conv_b[j]),
                    gate_group_weights(ev_w_r[j, direction], ev_w_i[j, direction]),
                    rows(ev_b_r[j, direction]), rows(ev_b_i[j, direction]),
                    rows(ev_lam[j, direction]), h_fwd, reverse=direction == 1)
            y_a = h_fwd.reshape(t, aw)
            y_b = dilated_mixture(qkv.reshape(bsz, seq, -1)).reshape(t, -1)
            ys, w_out = [y_a, y_b], ev_w_out[j]
        else:
            lam_init = 0.8 - 0.6 * math.exp(-0.3 * layer)
            w_in = od_w_in[j]
            qk = qkv_proj(xf, g_pre, w_in, 0, 2 * d, rope, seq, 2)
            vt = norm_matmul_t(xf, g_pre, w_in, 2 * d, d, bsz, seq, _pick(seq, (DIFF_TK, LANES)))
            y = diff_attention(qk.reshape(bsz, seq, -1), vt, od_lam_q1[j], od_lam_k1[j],
                               od_lam_q2[j], od_lam_k2[j], od_subln[j], lam_init)
            ys, w_out = [y.reshape(t, -1)], od_w_out[j]
        xf = out_proj_residual(ys, w_out, xf, gain(norm_mix_post[layer]))
        xf = ffn_residual(xf, gain(norm_ffn_pre[layer]), gain(norm_ffn_post[layer]),
                          ffn_w_gate[layer], ffn_w_up[layer], ffn_w_down[layer])
    return xf.reshape(bsz, seq, d)
```

```python
import functools
import math

import jax
import jax.numpy as jnp
from jax import lax
from jax.experimental import pallas as pl
from jax.experimental.pallas import tpu as pltpu

F32 = jnp.float32
BF16 = jnp.bfloat16

EPS = 1e-6
NEG = -1e30
HEAD_DIM = 64
ROT_DIM = HEAD_DIM // 4
ROT_HALF = ROT_DIM // 2
ROPE_THETA = 500000.0
CONV_WIDTH = 4
RG_C = 8.0
DILATION_PAIRS = ((128, 1), (512, 4), (2048, 16))
GELU_C = math.sqrt(2.0 / math.pi)

LANES = 128
SUBLANES = 8
BF16_SUBLANES = 16
MXU_DIM = 256
V7X_VMEM_BYTES = 64 * 1024 * 1024
VMEM_LIMIT_BYTES = V7X_VMEM_BYTES * 7 // 8

QKV_MAX_TN = 4096
DIFF_TK = MXU_DIM
DIFF_QK_SPAN = 4
ONES_ROWS = BF16_SUBLANES

LOG2E = math.log2(math.e)
Q_SCALE = HEAD_DIM ** -0.5 * LOG2E

_NT = (((1,), (1,)), ((), ()))


def _pick(n, prefs):
    for p in prefs:
        if n % p == 0:
            return p
    return n


def _cparams(sem):
    return pltpu.CompilerParams(dimension_semantics=sem, vmem_limit_bytes=VMEM_LIMIT_BYTES)


def _rms(x, g):
    return x * lax.rsqrt(jnp.mean(x * x, axis=-1, keepdims=True) + EPS) * g


def _norm_matmul_kernel(x_ref, g_ref, w_ref, o_ref):
    h = _rms(x_ref[...], g_ref[...]).astype(BF16)
    o_ref[...] = jnp.dot(h, w_ref[...], preferred_element_type=F32).astype(o_ref.dtype)


def _col_block(col0, tn):
    assert col0 % tn == 0
    return col0 // tn


def norm_matmul(x, g, w, col0, n, out_dtype):
    t, d = x.shape
    tm = _pick(t, (512, 256, 128))
    tn = _pick(n, (2048, 1024, 512, 256, 128))
    j0 = _col_block(col0, tn)
    return pl.pallas_call(
        _norm_matmul_kernel,
        grid=(t // tm, n // tn),
        in_specs=[pl.BlockSpec((tm, d), lambda i, j: (i, 0)),
                  pl.BlockSpec((1, d), lambda i, j: (0, 0)),
                  pl.BlockSpec((d, tn), lambda i, j: (0, j0 + j))],
        out_specs=pl.BlockSpec((tm, tn), lambda i, j: (i, j)),
        out_shape=jax.ShapeDtypeStruct((t, n), out_dtype),
        compiler_params=_cparams(("parallel", "parallel")),
        name="norm_matmul",
    )(x, g, w)


def _qkv_kernel(x_ref, g_ref, w_ref, c_ref, sa_ref, sb_ref, o_ref, *, width):
    h = _rms(x_ref[...], g_ref[...]).astype(BF16)
    acc = jnp.dot(h, w_ref[...], preferred_element_type=F32)
    tn = acc.shape[1]
    col0 = pl.program_id(1) * tn
    for cb in range(tn // LANES):
        part = (col0 + cb * LANES) // width
        a = acc[:, cb * LANES:(cb + 1) * LANES]
        r = (a * c_ref[part] + pltpu.roll(a, LANES - ROT_HALF, 1) * sa_ref[part]
             + pltpu.roll(a, ROT_HALF, 1) * sb_ref[part])
        o_ref[:, cb * LANES:(cb + 1) * LANES] = r.astype(o_ref.dtype)


def qkv_proj(x, g, w, col0, n, rope, seq, n_parts):
    t, d = x.shape
    width = n // n_parts
    tm = _pick(seq, (512, 256, 128))
    tn = n if n <= QKV_MAX_TN else _pick(width, (2048, 1024, 512, 256, 128))
    j0 = _col_block(col0, tn)
    nblk_s = seq // tm
    tab_spec = pl.BlockSpec((3, tm, LANES), lambda i, j: (0, i % nblk_s, 0))
    return pl.pallas_call(
        functools.partial(_qkv_kernel, width=width),
        grid=(t // tm, n // tn),
        in_specs=[pl.BlockSpec((tm, d), lambda i, j: (i, 0)),
                  pl.BlockSpec((1, d), lambda i, j: (0, 0)),
                  pl.BlockSpec((d, tn), lambda i, j: (0, j0 + j)),
                  tab_spec, tab_spec, tab_spec],
        out_specs=pl.BlockSpec((tm, tn), lambda i, j: (i, j)),
        out_shape=jax.ShapeDtypeStruct((t, n), BF16),
        compiler_params=_cparams(("parallel", "parallel")),
        name="qkv_proj",
    )(x, g, w, *rope)


def rope_tables(seq):
    pos = jnp.arange(seq, dtype=F32)
    inv = ROPE_THETA ** (-jnp.arange(0, ROT_DIM, 2, dtype=F32) / ROT_DIM)
    ang = pos[:, None] * inv[None, :]
    cos, sin = jnp.cos(ang), jnp.sin(ang)
    zeros = jnp.zeros((seq, HEAD_DIM - ROT_DIM), F32)
    zh = jnp.zeros((seq, ROT_HALF), F32)
    c = jnp.concatenate([cos, cos, zeros + 1.0], axis=1)
    sa = jnp.concatenate([-sin, zh, zeros], axis=1)
    sb = jnp.concatenate([zh, sin, zeros], axis=1)
    rep = LANES // HEAD_DIM
    ident = (jnp.ones_like(c), jnp.zeros_like(c), jnp.zeros_like(c))
    return tuple(jnp.tile(jnp.stack([tab * Q_SCALE, tab, idt]), (1, 1, rep))
                 for tab, idt in zip((c, sa, sb), ident))


def _rglru_kernel(*refs, reverse, final, n_blk):
    if final:
        (xp_ref, xc_ref, xn_ref, cw_ref, cb_ref, wg_ref, br_ref, bi_ref, lam_ref,
         gate_ref, prev_ref, o_ref, carry_ref) = refs
    else:
        (xp_ref, xc_ref, xn_ref, cw_ref, cb_ref, wg_ref, br_ref, bi_ref, lam_ref,
         o_ref, carry_ref) = refs
    i = pl.program_id(1)
    blk = n_blk - 1 - i if reverse else i
    tb, aw = xc_ref.shape

    @pl.when(i == 0)
    def _():
        carry_ref[...] = jnp.zeros_like(carry_ref)

    x0 = xc_ref[...]
    xp = jnp.where(blk > 0, xp_ref[...], 0.0)
    xn = jnp.where(blk < n_blk - 1, xn_ref[...], 0.0)
    row = lax.broadcasted_iota(jnp.int32, (tb, aw), 0)
    zpad = jnp.zeros((tb - SUBLANES, aw), F32)
    conv = cb_ref[...] + cw_ref[2:3, :] * x0
    for back in (1, 2):
        head = jnp.concatenate([pltpu.roll(xp, back, 0), zpad], axis=0)
        shifted = jnp.where(row < back, head, pltpu.roll(x0, back, 0))
        conv = conv + cw_ref[2 - back:3 - back, :] * shifted
    tail = jnp.concatenate([zpad, pltpu.roll(xn, SUBLANES - 1, 0)], axis=0)
    shifted = jnp.where(row == tb - 1, tail, pltpu.roll(x0, tb - 1, 0))
    conv = conv + cw_ref[3:4, :] * shifted

    n_grp, gw, _ = wg_ref.shape
    cbf = conv.astype(BF16)
    zr, zi = [], []
    for g in range(n_grp):
        z = jnp.dot(cbf[:, g * gw:(g + 1) * gw], wg_ref[g], preferred_element_type=F32)
        zr.append(z[:, :gw])
        zi.append(z[:, gw:])
    zr = jnp.concatenate(zr, axis=1) if n_grp > 1 else zr[0]
    zi = jnp.concatenate(zi, axis=1) if n_grp > 1 else zi[0]
    r = 1.0 / (1.0 + jnp.exp(-(zr + br_ref[...])))
    ig = 1.0 / (1.0 + jnp.exp(-(zi + bi_ref[...])))
    nl = -lam_ref[...]
    softplus = jnp.maximum(nl, 0.0) + jnp.log(1.0 + jnp.exp(-jnp.abs(nl)))
    a = jnp.exp(-RG_C * r * softplus)
    u = jnp.sqrt(jnp.maximum(1.0 - a * a, 0.0)) * (ig * conv)

    n_grp8 = tb // SUBLANES
    a3 = a.reshape(n_grp8, SUBLANES, aw)
    u3 = u.reshape(n_grp8, SUBLANES, aw)
    sub = lax.broadcasted_iota(jnp.int32, a3.shape, 1)
    s = 1
    while s < SUBLANES:
        if reverse:
            keep = sub < SUBLANES - s
            a_sh = jnp.where(keep, pltpu.roll(a3, SUBLANES - s, 1), 1.0)
            u_sh = jnp.where(keep, pltpu.roll(u3, SUBLANES - s, 1), 0.0)
        else:
            keep = sub >= s
            a_sh = jnp.where(keep, pltpu.roll(a3, s, 1), 1.0)
            u_sh = jnp.where(keep, pltpu.roll(u3, s, 1), 0.0)
        u3 = a3 * u_sh + u3
        a3 = a3 * a_sh
        s *= 2
    edge = 0 if reverse else SUBLANES - 1
    state = carry_ref[edge:edge + 1, :]
    groups = [None] * n_grp8
    for j in (range(n_grp8 - 1, -1, -1) if reverse else range(n_grp8)):
        groups[j] = a3[j] * state + u3[j]
        state = groups[j][edge:edge + 1, :]
    h = jnp.concatenate(groups, axis=0)
    carry_ref[...] = groups[0] if reverse else groups[-1]

    if final:
        gt = gate_ref[...]
        gelu = 0.5 * gt * (1.0 + jnp.tanh(GELU_C * (gt + 0.044715 * (gt * gt * gt))))
        o_ref[...] = ((prev_ref[...] + h) * gelu).astype(o_ref.dtype)
    else:
        o_ref[...] = h


def rglru_direction(z_a, conv_w, conv_b, w_gate, b_r, b_i, lam, prev, reverse):
    bsz, seq, two_aw = z_a.shape
    aw = two_aw // 2
    assert conv_w.shape == (CONV_WIDTH, aw) and CONV_WIDTH == 4
    tb = _pick(seq, (256, 128, 64, 32, 16, 8))
    n_blk = seq // tb
    rows8 = seq // SUBLANES
    per8 = tb // SUBLANES
    final = prev is not None

    def blk(i):
        return n_blk - 1 - i if reverse else i

    xr_specs = [
        pl.BlockSpec((None, SUBLANES, aw), lambda b, i: (b, jnp.maximum(blk(i) * per8 - 1, 0), 1)),
        pl.BlockSpec((None, tb, aw), lambda b, i: (b, blk(i), 1)),
        pl.BlockSpec((None, SUBLANES, aw),
                     lambda b, i: (b, jnp.minimum((blk(i) + 1) * per8, rows8 - 1), 1)),
    ]
    full = lambda a: pl.BlockSpec(a.shape, lambda b, i: (0,) * a.ndim)
    params = [conv_w, conv_b, w_gate, b_r, b_i, lam]
    in_specs = xr_specs + [full(p) for p in params]
    args = [z_a, z_a, z_a] + params
    if final:
        in_specs += [pl.BlockSpec((None, tb, aw), lambda b, i: (b, blk(i), 0)),
                     pl.BlockSpec((None, tb, aw), lambda b, i: (b, blk(i), 0))]
        args += [z_a, prev]
    return pl.pallas_call(
        functools.partial(_rglru_kernel, reverse=reverse, final=final, n_blk=n_blk),
        grid=(bsz, n_blk),
        in_specs=in_specs,
        out_specs=pl.BlockSpec((None, tb, aw), lambda b, i: (b, blk(i), 0)),
        out_shape=jax.ShapeDtypeStruct((bsz, seq, aw), BF16 if final else F32),
        scratch_shapes=[pltpu.VMEM((SUBLANES, aw), F32)],
        compiler_params=_cparams(("parallel", "arbitrary")),
        name="rglru_bwd" if reverse else "rglru_fwd",
    )(*args)


def gate_group_weights(w_r, w_i):
    n, bs, _ = w_r.shape
    aw = n * bs
    gw = _pick(aw, (MXU_DIM, LANES))
    per = gw // bs
    n_grp = aw // gw
    eye = jnp.eye(per, dtype=w_r.dtype)

    def dense(w):
        wg = w.reshape(n_grp, per, bs, bs)
        return jnp.einsum('gaef,ab->gaebf', wg, eye).reshape(n_grp, gw, gw)

    return jnp.concatenate([dense(w_r), dense(w_i)], axis=2).astype(BF16)


DIL_TILE = 128
DIL_UNROLL = 16


def _dilated_kernel(q_ref, kp_ref, kc_ref, kn_ref, vp_ref, vc_ref, vn_ref, o_ref,
                    qbuf, kbuf, vbuf, so_ref, sl_ref, *, seq, half, dilations):
    tb = q_ref.shape[0]
    halo = kp_ref.shape[0]
    t0 = pl.program_id(1) * tb
    qbuf[...] = q_ref[...].astype(F32)
    for buf, prev, cur, nxt in ((kbuf, kp_ref, kc_ref, kn_ref), (vbuf, vp_ref, vc_ref, vn_ref)):
        buf[0:halo] = prev[...].astype(F32)
        buf[halo:halo + tb] = cur[...].astype(F32)
        buf[halo + tb:halo + tb + halo] = nxt[...].astype(F32)

    nk = DIL_TILE + 2 * half
    rowq = lax.broadcasted_iota(jnp.int32, (DIL_TILE, nk), 0)
    colk = lax.broadcasted_iota(jnp.int32, (DIL_TILE, nk), 1)
    band_bias = jnp.where(jnp.abs(colk - half - rowq) <= half, 0.0, NEG).astype(F32)
    col1 = lax.broadcasted_iota(jnp.int32, (1, nk), 1)
    lane = lax.broadcasted_iota(jnp.int32, (DIL_TILE, LANES), 1)
    lo = lane < HEAD_DIM
    lo_k = lax.broadcasted_iota(jnp.int32, (nk, LANES), 1) < HEAD_DIM

    for bi, d in enumerate(dilations):
        span = DIL_TILE * d

        def rows(ref, start, size, d=d):
            if d == 1:
                return ref[pl.ds(pl.multiple_of(start, SUBLANES), size), :]
            return ref[pl.ds(start, size, stride=d), :]

        def tile(n, carry, bi=bi, d=d, span=span, rows=rows):
            g, r = n // d, n % d
            qs = g * span + r
            ks = halo + qs - half * d
            tok = (t0 - halo + ks) + col1 * d
            bias = band_bias + jnp.where((tok >= 0) & (tok < seq), 0.0, NEG)
            q = rows(qbuf, qs, DIL_TILE).astype(BF16)
            k = rows(kbuf, ks, nk).astype(BF16)
            v = rows(vbuf, ks, nk).astype(BF16)
            res, ms = None, []
            for first in (True, False):
                own = lo if first else ~lo
                qm = jnp.where(own, q, jnp.zeros_like(q))
                s = lax.dot_general(qm, k, _NT, preferred_element_type=F32) + bias
                m = jnp.max(s, axis=1, keepdims=True)
                p = jnp.exp2((s - m).astype(BF16))
                own_k = lo_k if first else ~lo_k
                v_ext = jnp.concatenate(
                    [jnp.where(own_k, v, jnp.zeros_like(v)),
                     jnp.where(own_k, 1.0, 0.0).astype(BF16)], axis=1)
                pv = jnp.dot(p, v_ext, preferred_element_type=F32)
                res = pv if res is None else res + pv
                ms.append(m)
            den = res[:, LANES:]
            o = res[:, :LANES] / den
            lse = jnp.where(lo, ms[0], ms[1]) + jnp.log2(den)
            if d == 1:
                dst = pl.ds(pl.multiple_of(qs, SUBLANES), DIL_TILE)
            else:
                dst = pl.ds(qs, DIL_TILE, stride=d)
            if bi > 0:
                lp = sl_ref[dst, :]
                mx = jnp.maximum(lp, lse)
                tot = mx + jnp.log2(jnp.exp2(lp - mx) + jnp.exp2(lse - mx))
                o = so_ref[dst, :] * jnp.exp2(lp - tot) + o * jnp.exp2(lse - tot)
                lse = tot
            so_ref[dst, :] = o
            if bi < len(dilations) - 1:
                sl_ref[dst, :] = lse
            return carry

        lax.fori_loop(0, tb // DIL_TILE, tile, 0, unroll=DIL_UNROLL)

    o_ref[...] = so_ref[...].astype(o_ref.dtype)


def dilated_mixture(qkv):
    bsz, seq, three_w = qkv.shape
    width = three_w // 3
    n_pairs = width // LANES
    dilations = tuple(sorted((d for _, d in DILATION_PAIRS), reverse=True))
    halves = {w // (2 * d) for w, d in DILATION_PAIRS}
    assert len(halves) == 1
    half = halves.pop()
    halo = half * max(dilations)
    tb = DIL_TILE * max(dilations)
    assert seq % tb == 0 and tb % halo == 0 and all(max(dilations) % d == 0 for d in dilations)
    per = tb // halo
    n_halo = seq // halo

    def centre(col):
        return pl.BlockSpec((None, tb, LANES), lambda b, i, p: (b, i, col * n_pairs + p))

    def before(col):
        return pl.BlockSpec((None, halo, LANES),
                            lambda b, i, p: (b, jnp.maximum(i * per - 1, 0), col * n_pairs + p))

    def after(col):
        return pl.BlockSpec((None, halo, LANES),
                            lambda b, i, p: (b, jnp.minimum((i + 1) * per, n_halo - 1),
                                             col * n_pairs + p))

    return pl.pallas_call(
        functools.partial(_dilated_kernel, seq=seq, half=half, dilations=dilations),
        grid=(bsz, seq // tb, n_pairs),
        in_specs=[centre(0), before(1), centre(1), after(1), before(2), centre(2), after(2)],
        out_specs=pl.BlockSpec((None, tb, LANES), lambda b, i, p: (b, i, p)),
        out_shape=jax.ShapeDtypeStruct((bsz, seq, width), BF16),
        scratch_shapes=[pltpu.VMEM((tb, LANES), F32),
                        pltpu.VMEM((tb + 2 * halo, LANES), F32),
                        pltpu.VMEM((tb + 2 * halo, LANES), F32),
                        pltpu.VMEM((tb, LANES), F32),
                        pltpu.VMEM((tb, LANES), F32)],
        compiler_params=_cparams(("parallel", "parallel", "parallel")),
        name="dilated_mixture",
    )(*([qkv] * 7))


def _norm_matmul_t_kernel(x_ref, g_ref, w_ref, o_ref):
    h = _rms(x_ref[...], g_ref[...]).astype(BF16)
    acc_t = jnp.dot(h, w_ref[...], preferred_element_type=F32).T
    n_chunk, _, tk = o_ref.shape
    for c in range(n_chunk):
        o_ref[c] = acc_t[:, c * tk:(c + 1) * tk].astype(o_ref.dtype)


def norm_matmul_t(x, g, w, col0, n, bsz, seq, tk):
    t, d = x.shape
    tm = _pick(seq, (512, 256, 128))
    tn = _pick(n, (2048, 1024, 512, 256, 128))
    j0 = _col_block(col0, tn)
    assert tm % tk == 0
    per = tm // tk
    nblk_s = seq // tm
    return pl.pallas_call(
        _norm_matmul_t_kernel,
        grid=(t // tm, n // tn),
        in_specs=[pl.BlockSpec((tm, d), lambda i, j: (i, 0)),
                  pl.BlockSpec((1, d), lambda i, j: (0, 0)),
                  pl.BlockSpec((d, tn), lambda i, j: (0, j0 + j))],
        out_specs=pl.BlockSpec((None, per, tn, tk), lambda i, j: (i // nblk_s, i % nblk_s, j, 0)),
        out_shape=jax.ShapeDtypeStruct((bsz, seq // tk, n, tk), BF16),
        compiler_params=_cparams(("parallel", "parallel")),
        name="norm_matmul_t",
    )(x, g, w)


def _diff_attn_kernel(q_ref, k_ref, vt_ref, lq1_ref, lk1_ref, lq2_ref, lk2_ref, sub_ref, o_ref,
                      sa_ref, sb_ref, acc_ref, *, lam_init, span):
    tq = q_ref.shape[0]
    n_kt, _, tk = vt_ref.shape
    q = q_ref[...]
    lane = lax.broadcasted_iota(jnp.int32, (tq, LANES), 1)
    lo = lane < HEAD_DIM
    q1 = jnp.where(lo, q, jnp.zeros_like(q))
    q2 = jnp.where(lo, jnp.zeros_like(q), q)

    n_qt = tq // MXU_DIM if tq % MXU_DIM == 0 else 1
    qt = tq // n_qt
    q_tiles = [(q1[n * qt:(n + 1) * qt], q2[n * qt:(n + 1) * qt]) for n in range(n_qt)]
    pieces = [(n, j) for n in range(n_qt) for j in range(2)]
    ones = jnp.ones((ONES_ROWS, tk), BF16)
    n_sp = n_kt // span

    def score_piece(sp, s_ref, n, j):
        k = k_ref[pl.ds(sp * span * tk, span * tk), :]
        st = lax.dot_general(k, q_tiles[n][j], _NT, preferred_element_type=F32)
        s_ref[j, :, n * qt:(n + 1) * qt] = st
        return [jnp.max(st[h * tk:(h + 1) * tk], axis=0, keepdims=True) for h in range(span)]

    def softmax_piece(s_ref, h, n, j, m, cm):
        m_new = jnp.maximum(m, cm)
        alpha = jnp.exp2(m - m_new)
        st = s_ref[j, h * tk:(h + 1) * tk, n * qt:(n + 1) * qt]
        return m_new, alpha, jnp.exp2((st - m_new).astype(BF16))

    def pv_piece(vt, n, j, alpha, p):
        cols = slice(n * qt, (n + 1) * qt)
        acc_ref[j, :, cols] = alpha * acc_ref[j, :, cols] + jnp.dot(vt, p, preferred_element_type=F32)

    bufs = (sa_ref, sb_ref)
    acc_ref[...] = jnp.zeros_like(acc_ref)
    ms = {p: jnp.full((1, qt), NEG, F32) for p in pieces}
    cms = {p: score_piece(0, sa_ref, *p) for p in pieces}
    for sp in range(n_sp):
        vts = [jnp.concatenate([vt_ref[sp * span + h], ones], axis=0) for h in range(span)]
        cms_next = {}
        for n, j in pieces:
            for h in range(span):
                ms[n, j], alpha, p = softmax_piece(bufs[sp % 2], h, n, j, ms[n, j], cms[n, j][h])
                pv_piece(vts[h], n, j, alpha, p)
                if sp + 1 < n_sp and h == 0:
                    cms_next[n, j] = score_piece(sp + 1, bufs[(sp + 1) % 2], n, j)
        cms = cms_next
    a1, l1 = acc_ref[0, :LANES], acc_ref[0, LANES:LANES + 1]
    a2, l2 = acc_ref[1, :LANES], acc_ref[1, LANES:LANES + 1]

    lam = (jnp.exp(jnp.sum(lq1_ref[...] * lk1_ref[...], axis=1, keepdims=True))
           - jnp.exp(jnp.sum(lq2_ref[...] * lk2_ref[...], axis=1, keepdims=True)) + lam_init)
    o_t = a1 * (1.0 / l1) - lam * (a2 * (1.0 / l2))
    o_t = o_t * (lax.rsqrt(jnp.mean(o_t * o_t, axis=0, keepdims=True) + EPS) * (1.0 - lam_init))
    o_ref[...] = (o_t.T * sub_ref[...]).astype(o_ref.dtype)


def diff_attention(qk, vt, lq1, lk1, lq2, lk2, subln, lam_init):
    bsz, seq, two_d = qk.shape
    d = two_d // 2
    n_heads = d // LANES
    n_kt, tk = vt.shape[1], vt.shape[3]
    span = _pick(n_kt, (DIFF_QK_SPAN, 1))
    tq = _pick(seq, (512, 256, 128))
    row = lambda a: a.reshape(1, -1).astype(F32)
    small = lambda n: pl.BlockSpec((1, n), lambda b, h, i: (0, 0))
    return pl.pallas_call(
        functools.partial(_diff_attn_kernel, lam_init=lam_init, span=span),
        grid=(bsz, n_heads, seq // tq),
        in_specs=[pl.BlockSpec((None, tq, LANES), lambda b, h, i: (b, i, h)),
                  pl.BlockSpec((None, seq, LANES), lambda b, h, i: (b, 0, n_heads + h)),
                  pl.BlockSpec((None, n_kt, LANES, tk), lambda b, h, i: (b, 0, h, 0)),
                  small(HEAD_DIM), small(HEAD_DIM), small(HEAD_DIM), small(HEAD_DIM),
                  small(LANES)],
        out_specs=pl.BlockSpec((None, tq, LANES), lambda b, h, i: (b, i, h)),
        out_shape=jax.ShapeDtypeStruct((bsz, seq, d), BF16),
        scratch_shapes=[pltpu.VMEM((2, span * tk, tq), F32), pltpu.VMEM((2, span * tk, tq), F32),
                        pltpu.VMEM((2, LANES + ONES_ROWS, tq), F32)],
        compiler_params=_cparams(("parallel", "parallel", "parallel")),
        name="diff_attention",
    )(qk, qk, vt, row(lq1), row(lk1), row(lq2), row(lk2), row(subln))


def _out_proj_kernel(*refs, n_in):
    ys = refs[:n_in]
    ws = refs[n_in:2 * n_in]
    x_ref, g_ref, o_ref = refs[2 * n_in:]
    m = jnp.dot(ys[0][...], ws[0][...], preferred_element_type=F32)
    for y_ref, w_ref in zip(ys[1:], ws[1:]):
        m = m + jnp.dot(y_ref[...], w_ref[...], preferred_element_type=F32)
    o_ref[...] = x_ref[...] + _rms(m, g_ref[...])


def out_proj_residual(ys, w, x, g):
    t, d = x.shape
    tm = _pick(t, (512, 256, 128))
    kdim = ys[0].shape[1]
    assert all(y.shape[1] == kdim for y in ys) and w.shape[0] == kdim * len(ys)
    in_specs = ([pl.BlockSpec((tm, kdim), lambda i: (i, 0)) for _ in ys]
                + [pl.BlockSpec((kdim, d), lambda i, r=r: (r, 0)) for r in range(len(ys))]
                + [pl.BlockSpec((tm, d), lambda i: (i, 0)), pl.BlockSpec((1, d), lambda i: (0, 0))])
    return pl.pallas_call(
        functools.partial(_out_proj_kernel, n_in=len(ys)),
        grid=(t // tm,),
        in_specs=in_specs,
        out_specs=pl.BlockSpec((tm, d), lambda i: (i, 0)),
        out_shape=jax.ShapeDtypeStruct((t, d), F32),
        compiler_params=_cparams(("parallel",)),
        name="out_proj",
    )(*ys, *([w] * len(ys)), x, g)


def _ffn_kernel(x_ref, gpre_ref, gpost_ref, wg_ref, wu_ref, wd_ref, o_ref, h_ref, acc_ref):
    f = pl.program_id(1)

    @pl.when(f == 0)
    def _():
        h_ref[...] = _rms(x_ref[...], gpre_ref[...]).astype(BF16)
        acc_ref[...] = jnp.zeros_like(acc_ref)

    h = h_ref[...]
    gate = jnp.dot(h, wg_ref[...], preferred_element_type=F32)
    up = jnp.dot(h, wu_ref[...], preferred_element_type=F32)
    act = (gate / (1.0 + jnp.exp(-gate))) * up
    acc_ref[...] += jnp.dot(act.astype(BF16), wd_ref[...], preferred_element_type=F32)

    @pl.when(f == pl.num_programs(1) - 1)
    def _():
        o_ref[...] = x_ref[...] + _rms(acc_ref[...], gpost_ref[...])


def ffn_residual(x, g_pre, g_post, w_gate, w_up, w_down):
    t, d = x.shape
    f = w_gate.shape[1]
    tm = _pick(t, (512, 256, 128))
    tf = _pick(f, (512, 256, 128))
    return pl.pallas_call(
        _ffn_kernel,
        grid=(t // tm, f // tf),
        in_specs=[pl.BlockSpec((tm, d), lambda i, j: (i, 0)),
                  pl.BlockSpec((1, d), lambda i, j: (0, 0)),
                  pl.BlockSpec((1, d), lambda i, j: (0, 0)),
                  pl.BlockSpec((d, tf), lambda i, j: (0, j)),
                  pl.BlockSpec((d, tf), lambda i, j: (0, j)),
                  pl.BlockSpec((tf, d), lambda i, j: (j, 0))],
        out_specs=pl.BlockSpec((tm, d), lambda i, j: (i, 0)),
        out_shape=jax.ShapeDtypeStruct((t, d), F32),
        scratch_shapes=[pltpu.VMEM((tm, d), BF16), pltpu.VMEM((tm, d), F32)],
        compiler_params=_cparams(("parallel", "arbitrary")),
        name="ffn",
    )(x, g_pre, g_post, w_gate, w_up, w_down)


def kernel(x, norm_mix_pre, norm_mix_post, norm_ffn_pre, norm_ffn_post, ev_w_in, ev_conv_w, ev_conv_b, ev_w_r, ev_b_r, ev_w_i, ev_b_i, ev_lam, ev_w_out, od_w_in, od_lam_q1, od_lam_k1, od_lam_q2, od_lam_k2, od_subln, od_w_out, ffn_w_gate, ffn_w_up, ffn_w_down):
    bsz, seq, d = x.shape
    depth = norm_mix_pre.shape[0]
    t = bsz * seq
    aw = ev_conv_w.shape[-1]
    rope = rope_tables(seq)
    gain = lambda g: g.reshape(1, d).astype(F32)
    ev_w_in, ev_w_out, od_w_in, od_w_out, ffn_w_gate, ffn_w_up, ffn_w_down = (
        w.astype(BF16) for w in (ev_w_in, ev_w_out, od_w_in, od_w_out,
                                 ffn_w_gate, ffn_w_up, ffn_w_down))
    xf = x.reshape(t, d)
    for layer in range(depth):
        j = layer // 2
        g_pre = gain(norm_mix_pre[layer])
        if layer % 2 == 0:
            w_in = ev_w_in[j]
            z_a = norm_matmul(xf, g_pre, w_in, 0, 2 * aw, F32).reshape(bsz, seq, 2 * aw)
            qkv = qkv_proj(xf, g_pre, w_in[:, 2 * aw:], 0, w_in.shape[1] - 2 * aw, rope, seq, 3)
            rows = lambda p: p.reshape(1, aw).astype(F32)
            h_fwd = None
            for direction in (0, 1):
                h_fwd = rglru_direction(
                    z_a, ev_conv_w[j].astype(F32), rows(ev_conv_b[j]),
                    gate_group_weights(ev_w_r[j, direction], ev_w_i[j, direction]),
                    rows(ev_b_r[j, direction]), rows(ev_b_i[j, direction]),
                    rows(ev_lam[j, direction]), h_fwd, reverse=direction == 1)
            y_a = h_fwd.reshape(t, aw)
            y_b = dilated_mixture(qkv.reshape(bsz, seq, -1)).reshape(t, -1)
            ys, w_out = [y_a, y_b], ev_w_out[j]
        else:
            lam_init = 0.8 - 0.6 * math.exp(-0.3 * layer)
            w_in = od_w_in[j]
            qk = qkv_proj(xf, g_pre, w_in, 0, 2 * d, rope, seq, 2)
            vt = norm_matmul_t(xf, g_pre, w_in, 2 * d, d, bsz, seq, _pick(seq, (DIFF_TK, LANES)))
            y = diff_attention(qk.reshape(bsz, seq, -1), vt, od_lam_q1[j], od_lam_k1[j],
                               od_lam_q2[j], od_lam_k2[j], od_subln[j], lam_init)
            ys, w_out = [y.reshape(t, -1)], od_w_out[j]
        xf = out_proj_residual(ys, w_out, xf, gain(norm_mix_post[layer]))
        xf = ffn_residual(xf, gain(norm_ffn_pre[layer]), gain(norm_ffn_post[layer]),
                          ffn_w_gate[layer], ffn_w_up[layer], ffn_w_down[layer])
    return xf.reshape(bsz, seq, d)
```

```python
import functools
import math

import jax
import jax.numpy as jnp
from jax import lax
from jax.experimental import pallas as pl
from jax.experimental.pallas import tpu as pltpu

F32 = jnp.float32
BF16 = jnp.bfloat16

EPS = 1e-6
NEG = -1e30
HEAD_DIM = 64
ROT_DIM = HEAD_DIM // 4
ROT_HALF = ROT_DIM // 2
ROPE_THETA = 500000.0
CONV_WIDTH = 4
RG_C = 8.0
DILATION_PAIRS = ((128, 1), (512, 4), (2048, 16))
GELU_C = math.sqrt(2.0 / math.pi)

LANES = 128
SUBLANES = 8
BF16_SUBLANES = 16
MXU_DIM = 256
V7X_VMEM_BYTES = 64 * 1024 * 1024
VMEM_LIMIT_BYTES = V7X_VMEM_BYTES * 7 // 8

QKV_MAX_TN = 4096
DIFF_TK = MXU_DIM
DIFF_QK_SPAN = 4
ONES_ROWS = BF16_SUBLANES

LOG2E = math.log2(math.e)
Q_SCALE = HEAD_DIM ** -0.5 * LOG2E

_NT = (((1,), (1,)), ((), ()))


def _pick(n, prefs):
    for p in prefs:
        if n % p == 0:
            return p
    return n


def _cparams(sem):
    return pltpu.CompilerParams(dimension_semantics=sem, vmem_limit_bytes=VMEM_LIMIT_BYTES)


def _rms(x, g):
    return x * lax.rsqrt(jnp.mean(x * x, axis=-1, keepdims=True) + EPS) * g


def _norm_matmul_kernel(x_ref, g_ref, w_ref, o_ref):
    h = _rms(x_ref[...], g_ref[...]).astype(BF16)
    o_ref[...] = jnp.dot(h, w_ref[...], preferred_element_type=F32).astype(o_ref.dtype)


def _col_block(col0, tn):
    assert col0 % tn == 0
    return col0 // tn


def norm_matmul(x, g, w, col0, n, out_dtype):
    t, d = x.shape
    tm = _pick(t, (512, 256, 128))
    tn = _pick(n, (2048, 1024, 512, 256, 128))
    j0 = _col_block(col0, tn)
    return pl.pallas_call(
        _norm_matmul_kernel,
        grid=(t // tm, n // tn),
        in_specs=[pl.BlockSpec((tm, d), lambda i, j: (i, 0)),
                  pl.BlockSpec((1, d), lambda i, j: (0, 0)),
                  pl.BlockSpec((d, tn), lambda i, j: (0, j0 + j))],
        out_specs=pl.BlockSpec((tm, tn), lambda i, j: (i, j)),
        out_shape=jax.ShapeDtypeStruct((t, n), out_dtype),
        compiler_params=_cparams(("parallel", "parallel")),
        name="norm_matmul",
    )(x, g, w)


def _qkv_kernel(x_ref, g_ref, w_ref, c_ref, sa_ref, sb_ref, o_ref, *, width):
    h = _rms(x_ref[...], g_ref[...]).astype(BF16)
    acc = jnp.dot(h, w_ref[...], preferred_element_type=F32)
    tn = acc.shape[1]
    col0 = pl.program_id(1) * tn
    for cb in range(tn // LANES):
        part = (col0 + cb * LANES) // width
        a = acc[:, cb * LANES:(cb + 1) * LANES]
        r = (a * c_ref[part] + pltpu.roll(a, LANES - ROT_HALF, 1) * sa_ref[part]
             + pltpu.roll(a, ROT_HALF, 1) * sb_ref[part])
        o_ref[:, cb * LANES:(cb + 1) * LANES] = r.astype(o_ref.dtype)


def qkv_proj(x, g, w, col0, n, rope, seq, n_parts):
    t, d = x.shape
    width = n // n_parts
    tm = _pick(seq, (512, 256, 128))
    tn = n if n <= QKV_MAX_TN else _pick(width, (2048, 1024, 512, 256, 128))
    j0 = _col_block(col0, tn)
    nblk_s = seq // tm
    tab_spec = pl.BlockSpec((3, tm, LANES), lambda i, j: (0, i % nblk_s, 0))
    return pl.pallas_call(
        functools.partial(_qkv_kernel, width=width),
        grid=(t // tm, n // tn),
        in_specs=[pl.BlockSpec((tm, d), lambda i, j: (i, 0)),
                  pl.BlockSpec((1, d), lambda i, j: (0, 0)),
                  pl.BlockSpec((d, tn), lambda i, j: (0, j0 + j)),
                  tab_spec, tab_spec, tab_spec],
        out_specs=pl.BlockSpec((tm, tn), lambda i, j: (i, j)),
        out_shape=jax.ShapeDtypeStruct((t, n), BF16),
        compiler_params=_cparams(("parallel", "parallel")),
        name="qkv_proj",
    )(x, g, w, *rope)


def rope_tables(seq):
    pos = jnp.arange(seq, dtype=F32)
    inv = ROPE_THETA ** (-jnp.arange(0, ROT_DIM, 2, dtype=F32) / ROT_DIM)
    ang = pos[:, None] * inv[None, :]
    cos, sin = jnp.cos(ang), jnp.sin(ang)
    zeros = jnp.zeros((seq, HEAD_DIM - ROT_DIM), F32)
    zh = jnp.zeros((seq, ROT_HALF), F32)
    c = jnp.concatenate([cos, cos, zeros + 1.0], axis=1)
    sa = jnp.concatenate([-sin, zh, zeros], axis=1)
    sb = jnp.concatenate([zh, sin, zeros], axis=1)
    rep = LANES // HEAD_DIM
    ident = (jnp.ones_like(c), jnp.zeros_like(c), jnp.zeros_like(c))
    return tuple(jnp.tile(jnp.stack([tab * Q_SCALE, tab, idt]), (1, 1, rep))
                 for tab, idt in zip((c, sa, sb), ident))


def _rglru_kernel(*refs, reverse, final, n_blk):
    if final:
        (xp_ref, xc_ref, xn_ref, cw_ref, cb_ref, wg_ref, br_ref, bi_ref, lam_ref,
         gate_ref, prev_ref, o_ref, carry_ref) = refs
    else:
        (xp_ref, xc_ref, xn_ref, cw_ref, cb_ref, wg_ref, br_ref, bi_ref, lam_ref,
         o_ref, carry_ref) = refs
    i = pl.program_id(1)
    blk = n_blk - 1 - i if reverse else i
    tb, aw = xc_ref.shape

    @pl.when(i == 0)
    def _():
        carry_ref[...] = jnp.zeros_like(carry_ref)

    x0 = xc_ref[...]
    xp = jnp.where(blk > 0, xp_ref[...], 0.0)
    xn = jnp.where(blk < n_blk - 1, xn_ref[...], 0.0)
    row = lax.broadcasted_iota(jnp.int32, (tb, aw), 0)
    zpad = jnp.zeros((tb - SUBLANES, aw), F32)
    conv = cb_ref[...] + cw_ref[2:3, :] * x0
    for back in (1, 2):
        head = jnp.concatenate([pltpu.roll(xp, back, 0), zpad], axis=0)
        shifted = jnp.where(row < back, head, pltpu.roll(x0, back, 0))
        conv = conv + cw_ref[2 - back:3 - back, :] * shifted
    tail = jnp.concatenate([zpad, pltpu.roll(xn, SUBLANES - 1, 0)], axis=0)
    shifted = jnp.where(row == tb - 1, tail, pltpu.roll(x0, tb - 1, 0))
    conv = conv + cw_ref[3:4, :] * shifted

    n_grp, gw, _ = wg_ref.shape
    cbf = conv.astype(BF16)
    zr, zi = [], []
    for g in range(n_grp):
        z = jnp.dot(cbf[:, g * gw:(g + 1) * gw], wg_ref[g], preferred_element_type=F32)
        zr.append(z[:, :gw])
        zi.append(z[:, gw:])
    zr = jnp.concatenate(zr, axis=1) if n_grp > 1 else zr[0]
    zi = jnp.concatenate(zi, axis=1) if n_grp > 1 else zi[0]
    r = 1.0 / (1.0 + jnp.exp(-(zr + br_ref[...])))
    ig = 1.0 / (1.0 + jnp.exp(-(zi + bi_ref[...])))
    nl = -lam_ref[...]
    softplus = jnp.maximum(nl, 0.0) + jnp.log(1.0 + jnp.exp(-jnp.abs(nl)))
    a = jnp.exp(-RG_C * r * softplus)
    u = jnp.sqrt(jnp.maximum(1.0 - a * a, 0.0)) * (ig * conv)

    n_grp8 = tb // SUBLANES
    a3 = a.reshape(n_grp8, SUBLANES, aw)
    u3 = u.reshape(n_grp8, SUBLANES, aw)
    sub = lax.broadcasted_iota(jnp.int32, a3.shape, 1)
    s = 1
    while s < SUBLANES:
        if reverse:
            keep = sub < SUBLANES - s
            a_sh = jnp.where(keep, pltpu.roll(a3, SUBLANES - s, 1), 1.0)
            u_sh = jnp.where(keep, pltpu.roll(u3, SUBLANES - s, 1), 0.0)
        else:
            keep = sub >= s
            a_sh = jnp.where(keep, pltpu.roll(a3, s, 1), 1.0)
            u_sh = jnp.where(keep, pltpu.roll(u3, s, 1), 0.0)
        u3 = a3 * u_sh + u3
        a3 = a3 * a_sh
        s *= 2
    edge = 0 if reverse else SUBLANES - 1
    state = carry_ref[edge:edge + 1, :]
    groups = [None] * n_grp8
    for j in (range(n_grp8 - 1, -1, -1) if reverse else range(n_grp8)):
        groups[j] = a3[j] * state + u3[j]
        state = groups[j][edge:edge + 1, :]
    h = jnp.concatenate(groups, axis=0)
    carry_ref[...] = groups[0] if reverse else groups[-1]

    if final:
        gt = gate_ref[...]
        gelu = 0.5 * gt * (1.0 + jnp.tanh(GELU_C * (gt + 0.044715 * (gt * gt * gt))))
        o_ref[...] = ((prev_ref[...] + h) * gelu).astype(o_ref.dtype)
    else:
        o_ref[...] = h


def rglru_direction(z_a, conv_w, conv_b, w_gate, b_r, b_i, lam, prev, reverse):
    bsz, seq, two_aw = z_a.shape
    aw = two_aw // 2
    assert conv_w.shape == (CONV_WIDTH, aw) and CONV_WIDTH == 4
    tb = _pick(seq, (256, 128, 64, 32, 16, 8))
    n_blk = seq // tb
    rows8 = seq // SUBLANES
    per8 = tb // SUBLANES
    final = prev is not None

    def blk(i):
        return n_blk - 1 - i if reverse else i

    xr_specs = [
        pl.BlockSpec((None, SUBLANES, aw), lambda b, i: (b, jnp.maximum(blk(i) * per8 - 1, 0), 1)),
        pl.BlockSpec((None, tb, aw), lambda b, i: (b, blk(i), 1)),
        pl.BlockSpec((None, SUBLANES, aw),
                     lambda b, i: (b, jnp.minimum((blk(i) + 1) * per8, rows8 - 1), 1)),
    ]
    full = lambda a: pl.BlockSpec(a.shape, lambda b, i: (0,) * a.ndim)
    params = [conv_w, conv_b, w_gate, b_r, b_i, lam]
    in_specs = xr_specs + [full(p) for p in params]
    args = [z_a, z_a, z_a] + params
    if final:
        in_specs += [pl.BlockSpec((None, tb, aw), lambda b, i: (b, blk(i), 0)),
                     pl.BlockSpec((None, tb, aw), lambda b, i: (b, blk(i), 0))]
        args += [z_a, prev]
    return pl.pallas_call(
        functools.partial(_rglru_kernel, reverse=reverse, final=final, n_blk=n_blk),
        grid=(bsz, n_blk),
        in_specs=in_specs,
        out_specs=pl.BlockSpec((None, tb, aw), lambda b, i: (b, blk(i), 0)),
        out_shape=jax.ShapeDtypeStruct((bsz, seq, aw), BF16 if final else F32),
        scratch_shapes=[pltpu.VMEM((SUBLANES, aw), F32)],
        compiler_params=_cparams(("parallel", "arbitrary")),
        name="rglru_bwd" if reverse else "rglru_fwd",
    )(*args)


def gate_group_weights(w_r, w_i):
    n, bs, _ = w_r.shape
    aw = n * bs
    gw = _pick(aw, (MXU_DIM, LANES))
    per = gw // bs
    n_grp = aw // gw
    eye = jnp.eye(per, dtype=w_r.dtype)

    def dense(w):
        wg = w.reshape(n_grp, per, bs, bs)
        return jnp.einsum('gaef,ab->gaebf', wg, eye).reshape(n_grp, gw, gw)

    return jnp.concatenate([dense(w_r), dense(w_i)], axis=2).astype(BF16)


DIL_TILE = 128
DIL_UNROLL = 16


def _dilated_kernel(q_ref, kp_ref, kc_ref, kn_ref, vp_ref, vc_ref, vn_ref, o_ref,
                    qbuf, kbuf, vbuf, so_ref, sl_ref, *, seq, half, dilations):
    tb = q_ref.shape[0]
    halo = kp_ref.shape[0]
    t0 = pl.program_id(1) * tb
    qbuf[...] = q_ref[...].astype(F32)
    for buf, prev, cur, nxt in ((kbuf, kp_ref, kc_ref, kn_ref), (vbuf, vp_ref, vc_ref, vn_ref)):
        buf[0:halo] = prev[...].astype(F32)
        buf[halo:halo + tb] = cur[...].astype(F32)
        buf[halo + tb:halo + tb + halo] = nxt[...].astype(F32)

    nk = DIL_TILE + 2 * half
    rowq = lax.broadcasted_iota(jnp.int32, (DIL_TILE, nk), 0)
    colk = lax.broadcasted_iota(jnp.int32, (DIL_TILE, nk), 1)
    band_bias = jnp.where(jnp.abs(colk - half - rowq) <= half, 0.0, NEG).astype(F32)
    col1 = lax.broadcasted_iota(jnp.int32, (1, nk), 1)
    lane = lax.broadcasted_iota(jnp.int32, (DIL_TILE, LANES), 1)
    lo = lane < HEAD_DIM
    lo_k = lax.broadcasted_iota(jnp.int32, (nk, LANES), 1) < HEAD_DIM

    for bi, d in enumerate(dilations):
        span = DIL_TILE * d

        def rows(ref, start, size, d=d):
            if d == 1:
                return ref[pl.ds(pl.multiple_of(start, SUBLANES), size), :]
            return ref[pl.ds(start, size, stride=d), :]

        def tile(n, carry, bi=bi, d=d, span=span, rows=rows):
            g, r = n // d, n % d
            qs = g * span + r
            ks = halo + qs - half * d
            tok = (t0 - halo + ks) + col1 * d
            bias = band_bias + jnp.where((tok >= 0) & (tok < seq), 0.0, NEG)
            q = rows(qbuf, qs, DIL_TILE).astype(BF16)
            k = rows(kbuf, ks, nk).astype(BF16)
            v = rows(vbuf, ks, nk).astype(BF16)
            res, ms = None, []
            for first in (True, False):
                own = lo if first else ~lo
                qm = jnp.where(own, q, jnp.zeros_like(q))
                s = lax.dot_general(qm, k, _NT, preferred_element_type=F32) + bias
                m = jnp.max(s, axis=1, keepdims=True)
                p = jnp.exp2((s - m).astype(BF16))
                own_k = lo_k if first else ~lo_k
                v_ext = jnp.concatenate(
                    [jnp.where(own_k, v, jnp.zeros_like(v)),
                     jnp.where(own_k, 1.0, 0.0).astype(BF16)], axis=1)
                pv = jnp.dot(p, v_ext, preferred_element_type=F32)
                res = pv if res is None else res + pv
                ms.append(m)
            den = res[:, LANES:]
            o = res[:, :LANES] / den
            lse = jnp.where(lo, ms[0], ms[1]) + jnp.log2(den)
            if d == 1:
                dst = pl.ds(pl.multiple_of(qs, SUBLANES), DIL_TILE)
            else:
                dst = pl.ds(qs, DIL_TILE, stride=d)
            if bi > 0:
                lp = sl_ref[dst, :]
                mx = jnp.maximum(lp, lse)
                tot = mx + jnp.log2(jnp.exp2(lp - mx) + jnp.exp2(lse - mx))
                o = so_ref[dst, :] * jnp.exp2(lp - tot) + o * jnp.exp2(lse - tot)
                lse = tot
            so_ref[dst, :] = o
            if bi < len(dilations) - 1:
                sl_ref[dst, :] = lse
            return carry

        lax.fori_loop(0, tb // DIL_TILE, tile, 0, unroll=DIL_UNROLL)

    o_ref[...] = so_ref[...].astype(o_ref.dtype)


def dilated_mixture(qkv):
    bsz, seq, three_w = qkv.shape
    width = three_w // 3
    n_pairs = width // LANES
    dilations = tuple(sorted((d for _, d in DILATION_PAIRS), reverse=True))
    halves = {w // (2 * d) for w, d in DILATION_PAIRS}
    assert len(halves) == 1
    half = halves.pop()
    halo = half * max(dilations)
    tb = DIL_TILE * max(dilations)
    assert seq % tb == 0 and tb % halo == 0 and all(max(dilations) % d == 0 for d in dilations)
    per = tb // halo
    n_halo = seq // halo

    def centre(col):
        return pl.BlockSpec((None, tb, LANES), lambda b, i, p: (b, i, col * n_pairs + p))

    def before(col):
        return pl.BlockSpec((None, halo, LANES),
                            lambda b, i, p: (b, jnp.maximum(i * per - 1, 0), col * n_pairs + p))

    def after(col):
        return pl.BlockSpec((None, halo, LANES),
                            lambda b, i, p: (b, jnp.minimum((i + 1) * per, n_halo - 1),
                                             col * n_pairs + p))

    return pl.pallas_call(
        functools.partial(_dilated_kernel, seq=seq, half=half, dilations=dilations),
        grid=(bsz, seq // tb, n_pairs),
        in_specs=[centre(0), before(1), centre(1), after(1), before(2), centre(2), after(2)],
        out_specs=pl.BlockSpec((None, tb, LANES), lambda b, i, p: (b, i, p)),
        out_shape=jax.ShapeDtypeStruct((bsz, seq, width), BF16),
        scratch_shapes=[pltpu.VMEM((tb, LANES), F32),
                        pltpu.VMEM((tb + 2 * halo, LANES), F32),
                        pltpu.VMEM((tb + 2 * halo, LANES), F32),
                        pltpu.VMEM((tb, LANES), F32),
                        pltpu.VMEM((tb, LANES), F32)],
        compiler_params=_cparams(("parallel", "parallel", "parallel")),
        name="dilated_mixture",
    )(*([qkv] * 7))


def _norm_matmul_t_kernel(x_ref, g_ref, w_ref, o_ref):
    h = _rms(x_ref[...], g_ref[...]).astype(BF16)
    acc_t = jnp.dot(h, w_ref[...], preferred_element_type=F32).T
    n_chunk, _, tk = o_ref.shape
    for c in range(n_chunk):
        o_ref[c] = acc_t[:, c * tk:(c + 1) * tk].astype(o_ref.dtype)


def norm_matmul_t(x, g, w, col0, n, bsz, seq, tk):
    t, d = x.shape
    tm = _pick(seq, (512, 256, 128))
    tn = _pick(n, (2048, 1024, 512, 256, 128))
    j0 = _col_block(col0, tn)
    assert tm % tk == 0
    per = tm // tk
    nblk_s = seq // tm
    return pl.pallas_call(
        _norm_matmul_t_kernel,
        grid=(t // tm, n // tn),
        in_specs=[pl.BlockSpec((tm, d), lambda i, j: (i, 0)),
                  pl.BlockSpec((1, d), lambda i, j: (0, 0)),
                  pl.BlockSpec((d, tn), lambda i, j: (0, j0 + j))],
        out_specs=pl.BlockSpec((None, per, tn, tk), lambda i, j: (i // nblk_s, i % nblk_s, j, 0)),
        out_shape=jax.ShapeDtypeStruct((bsz, seq // tk, n, tk), BF16),
        compiler_params=_cparams(("parallel", "parallel")),
        name="norm_matmul_t",
    )(x, g, w)


def _diff_attn_kernel(q_ref, k_ref, vt_ref, lq1_ref, lk1_ref, lq2_ref, lk2_ref, sub_ref, o_ref,
                      sa_ref, sb_ref, acc_ref, *, lam_init, span):
    tq = q_ref.shape[0]
    n_kt, _, tk = vt_ref.shape
    q = q_ref[...]
    lane = lax.broadcasted_iota(jnp.int32, (tq, LANES), 1)
    lo = lane < HEAD_DIM
    q1 = jnp.where(lo, q, jnp.zeros_like(q))
    q2 = jnp.where(lo, jnp.zeros_like(q), q)

    n_qt = tq // MXU_DIM if tq % MXU_DIM == 0 else 1
    qt = tq // n_qt
    q_tiles = [(q1[n * qt:(n + 1) * qt], q2[n * qt:(n + 1) * qt]) for n in range(n_qt)]
    pieces = [(n, j) for n in range(n_qt) for j in range(2)]
    ones = jnp.ones((ONES_ROWS, tk), BF16)
    n_sp = n_kt // span

    def score_piece(sp, s_ref, n, j):
        k = k_ref[pl.ds(sp * span * tk, span * tk), :]
        st = lax.dot_general(k, q_tiles[n][j], _NT, preferred_element_type=F32)
        s_ref[j, :, n * qt:(n + 1) * qt] = st
        return [jnp.max(st[h * tk:(h + 1) * tk], axis=0, keepdims=True) for h in range(span)]

    def softmax_piece(s_ref, h, n, j, m, cm):
        m_new = jnp.maximum(m, cm)
        alpha = jnp.exp2(m - m_new)
        st = s_ref[j, h * tk:(h + 1) * tk, n * qt:(n + 1) * qt]
        return m_new, alpha, jnp.exp2((st - m_new).astype(BF16))

    def pv_piece(vt, n, j, alpha, p, first):
        cols = slice(n * qt, (n + 1) * qt)
        pv = jnp.dot(vt, p, preferred_element_type=F32)
        acc_ref[j, :, cols] = pv if first else alpha * acc_ref[j, :, cols] + pv

    bufs = (sa_ref, sb_ref)
    ms = {p: jnp.full((1, qt), NEG, F32) for p in pieces}
    cms = {p: score_piece(0, sa_ref, *p) for p in pieces}
    for sp in range(n_sp):
        vts = [jnp.concatenate([vt_ref[sp * span + h], ones], axis=0) for h in range(span)]
        cms_next = {}
        for n, j in pieces:
            for h in range(span):
                ms[n, j], alpha, p = softmax_piece(bufs[sp % 2], h, n, j, ms[n, j], cms[n, j][h])
                pv_piece(vts[h], n, j, alpha, p, first=sp == 0 and h == 0)
                if sp + 1 < n_sp and h == 0:
                    cms_next[n, j] = score_piece(sp + 1, bufs[(sp + 1) % 2], n, j)
        cms = cms_next
    a1, l1 = acc_ref[0, :LANES], acc_ref[0, LANES:LANES + 1]
    a2, l2 = acc_ref[1, :LANES], acc_ref[1, LANES:LANES + 1]

    lam = (jnp.exp(jnp.sum(lq1_ref[...] * lk1_ref[...], axis=1, keepdims=True))
           - jnp.exp(jnp.sum(lq2_ref[...] * lk2_ref[...], axis=1, keepdims=True)) + lam_init)
    o_t = a1 * (1.0 / l1) - lam * (a2 * (1.0 / l2))
    o_t = o_t * (lax.rsqrt(jnp.mean(o_t * o_t, axis=0, keepdims=True) + EPS) * (1.0 - lam_init))
    o_ref[...] = (o_t.T * sub_ref[...]).astype(o_ref.dtype)


def diff_attention(qk, vt, lq1, lk1, lq2, lk2, subln, lam_init):
    bsz, seq, two_d = qk.shape
    d = two_d // 2
    n_heads = d // LANES
    n_kt, tk = vt.shape[1], vt.shape[3]
    span = _pick(n_kt, (DIFF_QK_SPAN, 1))
    tq = _pick(seq, (512, 256, 128))
    row = lambda a: a.reshape(1, -1).astype(F32)
    small = lambda n: pl.BlockSpec((1, n), lambda b, h, i: (0, 0))
    return pl.pallas_call(
        functools.partial(_diff_attn_kernel, lam_init=lam_init, span=span),
        grid=(bsz, n_heads, seq // tq),
        in_specs=[pl.BlockSpec((None, tq, LANES), lambda b, h, i: (b, i, h)),
                  pl.BlockSpec((None, seq, LANES), lambda b, h, i: (b, 0, n_heads + h)),
                  pl.BlockSpec((None, n_kt, LANES, tk), lambda b, h, i: (b, 0, h, 0)),
                  small(HEAD_DIM), small(HEAD_DIM), small(HEAD_DIM), small(HEAD_DIM),
                  small(LANES)],
        out_specs=pl.BlockSpec((None, tq, LANES), lambda b, h, i: (b, i, h)),
        out_shape=jax.ShapeDtypeStruct((bsz, seq, d), BF16),
        scratch_shapes=[pltpu.VMEM((2, span * tk, tq), F32), pltpu.VMEM((2, span * tk, tq), F32),
                        pltpu.VMEM((2, LANES + ONES_ROWS, tq), F32)],
        compiler_params=_cparams(("parallel", "parallel", "parallel")),
        name="diff_attention",
    )(qk, qk, vt, row(lq1), row(lk1), row(lq2), row(lk2), row(subln))


def _out_proj_kernel(*refs, n_in):
    ys = refs[:n_in]
    ws = refs[n_in:2 * n_in]
    x_ref, g_ref, o_ref = refs[2 * n_in:]
    m = jnp.dot(ys[0][...], ws[0][...], preferred_element_type=F32)
    for y_ref, w_ref in zip(ys[1:], ws[1:]):
        m = m + jnp.dot(y_ref[...], w_ref[...], preferred_element_type=F32)
    o_ref[...] = x_ref[...] + _rms(m, g_ref[...])


def out_proj_residual(ys, w, x, g):
    t, d = x.shape
    tm = _pick(t, (512, 256, 128))
    kdim = ys[0].shape[1]
    assert all(y.shape[1] == kdim for y in ys) and w.shape[0] == kdim * len(ys)
    in_specs = ([pl.BlockSpec((tm, kdim), lambda i: (i, 0)) for _ in ys]
                + [pl.BlockSpec((kdim, d), lambda i, r=r: (r, 0)) for r in range(len(ys))]
                + [pl.BlockSpec((tm, d), lambda i: (i, 0)), pl.BlockSpec((1, d), lambda i: (0, 0))])
    return pl.pallas_call(
        functools.partial(_out_proj_kernel, n_in=len(ys)),
        grid=(t // tm,),
        in_specs=in_specs,
        out_specs=pl.BlockSpec((tm, d), lambda i: (i, 0)),
        out_shape=jax.ShapeDtypeStruct((t, d), F32),
        compiler_params=_cparams(("parallel",)),
        name="out_proj",
    )(*ys, *([w] * len(ys)), x, g)


def _ffn_kernel(x_ref, gpre_ref, gpost_ref, wg_ref, wu_ref, wd_ref, o_ref, h_ref, acc_ref):
    f = pl.program_id(1)

    @pl.when(f == 0)
    def _():
        h_ref[...] = _rms(x_ref[...], gpre_ref[...]).astype(BF16)
        acc_ref[...] = jnp.zeros_like(acc_ref)

    h = h_ref[...]
    gate = jnp.dot(h, wg_ref[...], preferred_element_type=F32)
    up = jnp.dot(h, wu_ref[...], preferred_element_type=F32)
    act = (gate / (1.0 + jnp.exp(-gate))) * up
    acc_ref[...] += jnp.dot(act.astype(BF16), wd_ref[...], preferred_element_type=F32)

    @pl.when(f == pl.num_programs(1) - 1)
    def _():
        o_ref[...] = x_ref[...] + _rms(acc_ref[...], gpost_ref[...])


def ffn_residual(x, g_pre, g_post, w_gate, w_up, w_down):
    t, d = x.shape
    f = w_gate.shape[1]
    tm = _pick(t, (512, 256, 128))
    tf = _pick(f, (512, 256, 128))
    return pl.pallas_call(
        _ffn_kernel,
        grid=(t // tm, f // tf),
        in_specs=[pl.BlockSpec((tm, d), lambda i, j: (i, 0)),
                  pl.BlockSpec((1, d), lambda i, j: (0, 0)),
                  pl.BlockSpec((1, d), lambda i, j: (0, 0)),
                  pl.BlockSpec((d, tf), lambda i, j: (0, j)),
                  pl.BlockSpec((d, tf), lambda i, j: (0, j)),
                  pl.BlockSpec((tf, d), lambda i, j: (j, 0))],
        out_specs=pl.BlockSpec((tm, d), lambda i, j: (i, 0)),
        out_shape=jax.ShapeDtypeStruct((t, d), F32),
        scratch_shapes=[pltpu.VMEM((tm, d), BF16), pltpu.VMEM((tm, d), F32)],
        compiler_params=_cparams(("parallel", "arbitrary")),
        name="ffn",
    )(x, g_pre, g_post, w_gate, w_up, w_down)


def kernel(x, norm_mix_pre, norm_mix_post, norm_ffn_pre, norm_ffn_post, ev_w_in, ev_conv_w, ev_conv_b, ev_w_r, ev_b_r, ev_w_i, ev_b_i, ev_lam, ev_w_out, od_w_in, od_lam_q1, od_lam_k1, od_lam_q2, od_lam_k2, od_subln, od_w_out, ffn_w_gate, ffn_w_up, ffn_w_down):
    bsz, seq, d = x.shape
    depth = norm_mix_pre.shape[0]
    t = bsz * seq
    aw = ev_conv_w.shape[-1]
    rope = rope_tables(seq)
    gain = lambda g: g.reshape(1, d).astype(F32)
    ev_w_in, ev_w_out, od_w_in, od_w_out, ffn_w_gate, ffn_w_up, ffn_w_down = (
        w.astype(BF16) for w in (ev_w_in, ev_w_out, od_w_in, od_w_out,
                                 ffn_w_gate, ffn_w_up, ffn_w_down))
    xf = x.reshape(t, d)
    for layer in range(depth):
        j = layer // 2
        g_pre = gain(norm_mix_pre[layer])
        if layer % 2 == 0:
            w_in = ev_w_in[j]
            z_a = norm_matmul(xf, g_pre, w_in, 0, 2 * aw, F32).reshape(bsz, seq, 2 * aw)
            qkv = qkv_proj(xf, g_pre, w_in[:, 2 * aw:], 0, w_in.shape[1] - 2 * aw, rope, seq, 3)
            rows = lambda p: p.reshape(1, aw).astype(F32)
            h_fwd = None
            for direction in (0, 1):
                h_fwd = rglru_direction(
                    z_a, ev_conv_w[j].astype(F32), rows(ev_conv_b[j]),
                    gate_group_weights(ev_w_r[j, direction], ev_w_i[j, direction]),
                    rows(ev_b_r[j, direction]), rows(ev_b_i[j, direction]),
                    rows(ev_lam[j, direction]), h_fwd, reverse=direction == 1)
            y_a = h_fwd.reshape(t, aw)
            y_b = dilated_mixture(qkv.reshape(bsz, seq, -1)).reshape(t, -1)
            ys, w_out = [y_a, y_b], ev_w_out[j]
        else:
            lam_init = 0.8 - 0.6 * math.exp(-0.3 * layer)
            w_in = od_w_in[j]
            qk = qkv_proj(xf, g_pre, w_in, 0, 2 * d, rope, seq, 2)
            vt = norm_matmul_t(xf, g_pre, w_in, 2 * d, d, bsz, seq, _pick(seq, (DIFF_TK, LANES)))
            y = diff_attention(qk.reshape(bsz, seq, -1), vt, od_lam_q1[j], od_lam_k1[j],
                               od_lam_q2[j], od_lam_k2[j], od_subln[j], lam_init)
            ys, w_out = [y.reshape(t, -1)], od_w_out[j]
        xf = out_proj_residual(ys, w_out, xf, gain(norm_mix_post[layer]))
        xf = ffn_residual(xf, gain(norm_ffn_pre[layer]), gain(norm_ffn_post[layer]),
                          ffn_w_gate[layer], ffn_w_up[layer], ffn_w_down[layer])
    return xf.reshape(bsz, seq, d)
```
